```python
import jax, jax.numpy as jnp
from jax import lax
import numpy as np

D_MODEL = 1024
BATCH = 16
SEQ = 4096
DEPTH = 2
DEC_BATCH = 16
DEC_SEQ = 32
PAST_LEN = 1024

CHUNK = 64
D_POOL = D_MODEL // 4
POOL_WINDOWS = (2, 4, 8, 16)
N_POOL_GROUPS = len(POOL_WINDOWS)
POOL_GROUP = D_POOL // N_POOL_GROUPS
POOL_MAX = max(POOL_WINDOWS)
POOL_HIST = POOL_MAX - 1
D_RWKV = D_MODEL - D_POOL
HEAD_DIM = 64
N_HEADS = D_RWKV // HEAD_DIM
LORA_W = 64
LORA_A = 64
LORA_G = 128
D_SHIFT = 3 * D_RWKV + LORA_W + LORA_A + LORA_G
N_IN = D_POOL + D_SHIFT
RWKV_SPLITS = (D_RWKV, 2 * D_RWKV, 3 * D_RWKV, 3 * D_RWKV + LORA_W, 3 * D_RWKV + LORA_W + LORA_A)
D_FF = 2816
D_PLE = 256
ALPHA = (2.0 * DEPTH) ** 0.25
BETA = (8.0 * DEPTH) ** -0.25
LN_EPS = 1e-5
LNX_EPS = 64e-5

kernel_name = "pool_rwkv7_macaron_deepnorm_stream"


def layer_norm(x, g, b, eps=LN_EPS):
    xf = x.astype(jnp.float32)
    mu = jnp.mean(xf, axis=-1, keepdims=True)
    var = jnp.mean(jnp.square(xf - mu), axis=-1, keepdims=True)
    return ((xf - mu) * lax.rsqrt(var + eps) * g.astype(jnp.float32) + b.astype(jnp.float32)).astype(x.dtype)


def swiglu(x, w_in, w_out):
    gate, up = jnp.split(x @ w_in, 2, axis=-1)
    return (jax.nn.silu(gate) * up) @ w_out


def pool_mixer(u, u_hist, pos0, pool_w, pool_scale):
    B, T, _ = u.shape
    H = u_hist.shape[1]
    u_ext = jnp.concatenate([u_hist, u], axis=1)
    padded = jnp.pad(u_ext.astype(jnp.float32), ((0, 0), (POOL_MAX, 0), (0, 0)))
    cs = jnp.cumsum(padded, axis=1)
    end = cs[:, H + POOL_MAX:H + POOL_MAX + T]
    pos = pos0 + jnp.arange(T)
    means = []
    for gi, w in enumerate(POOL_WINDOWS):
        sl = slice(gi * POOL_GROUP, (gi + 1) * POOL_GROUP)
        s = end[..., sl] - cs[:, H + POOL_MAX - w:H + POOL_MAX - w + T, sl]
        cnt = jnp.minimum(pos + 1, w).astype(jnp.float32)[None, :, None]
        means.append(s / cnt)
    pooled = jnp.concatenate(means, axis=-1).astype(u.dtype) - u
    mixed = jnp.einsum('btgc,gcd->btgd', pooled.reshape(B, T, N_POOL_GROUPS, POOL_GROUP), pool_w)
    out = mixed.reshape(B, T, D_POOL) * pool_scale
    return out, u_ext[:, -POOL_HIST:]


def wkv_step(S, inp):
    r_t, w_t, k_t, v_t, a_t, b_t = inp
    sa = jnp.einsum('bhvk,bhk->bhv', S, a_t)
    S = S * w_t[:, :, None, :] + sa[..., None] * b_t[:, :, None, :] + v_t[..., None] * k_t[:, :, None, :]
    y = jnp.einsum('bhvk,bhk->bhv', S, r_t)
    return S, y


def rwkv7_mixer(zr, shift_prev, wkv_prev, mu_shift, w0, w2, a0, a2, g2, k_k, k_a, r_k, lnx_g, lnx_b):
    B, T, _ = zr.shape
    z_prev = jnp.concatenate([shift_prev[:, None, :], zr[:, :-1]], axis=1)
    zs = zr + (z_prev - zr) * mu_shift
    r, k, v, lw, la, lg = jnp.split(zs, RWKV_SPLITS, axis=-1)
    w = -jax.nn.softplus(-(w0 + jnp.tanh(lw) @ w2)) - 0.5
    decay = jnp.exp(-jnp.exp(w.astype(jnp.float32)))
    a = jax.nn.sigmoid(a0 + la @ a2)
    g = jax.nn.sigmoid(lg) @ g2

    def heads(t):
        return t.reshape(B, T, N_HEADS, HEAD_DIM).astype(jnp.float32)

    kk = heads(k * k_k)
    kk = kk / jnp.maximum(jnp.sqrt(jnp.sum(kk * kk, axis=-1, keepdims=True)), 1e-12)
    k = k * (1 + (a - 1) * k_a)
    rh, kh, vh, ah, wh = heads(r), heads(k), heads(v), heads(a), heads(decay)
    bh = kk * ah

    def tm(t):
        return jnp.moveaxis(t, 1, 0)

    S_final, ys = lax.scan(wkv_step, wkv_prev.astype(jnp.float32),
                           (tm(rh), tm(wh), tm(kh), tm(vh), tm(-kk), tm(bh)))
    ys = jnp.moveaxis(ys, 0, 1)
    ys = layer_norm(ys, lnx_g.reshape(N_HEADS, HEAD_DIM), lnx_b.reshape(N_HEADS, HEAD_DIM), LNX_EPS)
    ys = ys + jnp.sum(rh * kh * r_k.astype(jnp.float32), axis=-1, keepdims=True) * vh
    out = ys.reshape(B, T, D_RWKV).astype(zr.dtype) * g
    return out, zr[:, -1], S_final.astype(wkv_prev.dtype)


def run_trunk(x, p, pos0, pool_hist, shift_prev, wkv_prev, W):
    pools, shifts, wkvs = [], [], []
    for i in range(DEPTH):
        x = layer_norm(ALPHA * x + 0.5 * swiglu(x, W['ffn1_w_in'][i], W['ffn1_w_out'][i]),
                       W['ln1_g'][i], W['ln1_b'][i])
        z = x @ W['w_in'][i]
        u, zr = z[..., :D_POOL], z[..., D_POOL:]
        pool_out, pool_new = pool_mixer(u, pool_hist[i], pos0, W['pool_w'][i], W['pool_scale'][i])
        rwkv_out, shift_new, wkv_new = rwkv7_mixer(
            zr, shift_prev[i], wkv_prev[i], W['mu_shift'][i], W['w0'][i], W['w2'][i], W['a0'][i],
            W['a2'][i], W['g2'][i], W['k_k'][i], W['k_a'][i], W['r_k'][i], W['lnx_g'][i], W['lnx_b'][i])
        mix = jnp.concatenate([pool_out, rwkv_out], axis=-1) @ W['w_out'][i]
        x = layer_norm(ALPHA * x + mix, W['ln2_g'][i], W['ln2_b'][i])
        h = ALPHA * x + 0.5 * swiglu(x, W['ffn2_w_in'][i], W['ffn2_w_out'][i])
        gate = jax.nn.sigmoid(h @ W['ple_gate_w'][i] + W['ple_gate_b'][i])
        h = h + gate * (p[i] @ W['ple_w'][i])
        x = layer_norm(h, W['ln3_g'][i], W['ln3_b'][i])
        pools.append(pool_new)
        shifts.append(shift_new)
        wkvs.append(wkv_new)
    return x, jnp.stack(pools), jnp.stack(shifts), jnp.stack(wkvs)


def setup_inputs(seed: int = 0) -> dict:
    key = jax.random.key(seed)
    ks = iter(jax.random.split(key, 48))
    f32 = jnp.float32

    def nrm(shape, scale):
        return jax.random.normal(next(ks), shape, f32) * scale

    def gain(shape):
        return 1.0 + nrm(shape, 0.05)

    L = DEPTH
    w0_base = jnp.tile(jnp.linspace(-6.0, 1.0, HEAD_DIM, dtype=f32), N_HEADS)
    return {
        "x_prompt": nrm((BATCH, SEQ, D_MODEL), 1.0),
        "x_sample": nrm((DEC_BATCH, DEC_SEQ, D_MODEL), 1.0),
        "p_prompt": nrm((DEPTH, BATCH, SEQ, D_PLE), 1.0),
        "p_sample": nrm((DEPTH, DEC_BATCH, DEC_SEQ, D_PLE), 1.0),
        "state_pool": nrm((DEPTH, DEC_BATCH, POOL_HIST, D_POOL), 1.0),
        "state_shift": nrm((DEPTH, DEC_BATCH, D_SHIFT), 1.0),
        "state_wkv": nrm((DEPTH, DEC_BATCH, N_HEADS, HEAD_DIM, HEAD_DIM), 0.3),
        "ln1_g": gain((L, D_MODEL)), "ln1_b": nrm((L, D_MODEL), 0.02),
        "ln2_g": gain((L, D_MODEL)), "ln2_b": nrm((L, D_MODEL), 0.02),
        "ln3_g": gain((L, D_MODEL)), "ln3_b": nrm((L, D_MODEL), 0.02),
        "ffn1_w_in": nrm((L, D_MODEL, 2 * D_FF), D_MODEL ** -0.5),
        "ffn1_w_out": nrm((L, D_FF, D_MODEL), BETA * D_FF ** -0.5),
        "ffn2_w_in": nrm((L, D_MODEL, 2 * D_FF), D_MODEL ** -0.5),
        "ffn2_w_out": nrm((L, D_FF, D_MODEL), BETA * D_FF ** -0.5),
        "w_in": nrm((L, D_MODEL, N_IN), D_MODEL ** -0.5),
        "mu_shift": jax.random.uniform(next(ks), (L, D_SHIFT), f32, 0.05, 0.95),
        "pool_w": nrm((L, N_POOL_GROUPS, POOL_GROUP, POOL_GROUP), POOL_GROUP ** -0.5),
        "pool_scale": gain((L, D_POOL)),
        "w0": w0_base[None, :] + nrm((L, D_RWKV), 0.1),
        "w2": nrm((L, LORA_W, D_RWKV), 0.1 * LORA_W ** -0.5),
        "a0": nrm((L, D_RWKV), 0.1),
        "a2": nrm((L, LORA_A, D_RWKV), 0.3 * LORA_A ** -0.5),
        "g2": nrm((L, LORA_G, D_RWKV), LORA_G ** -0.5),
        "k_k": 0.85 + nrm((L, D_RWKV), 0.05),
        "k_a": gain((L, D_RWKV)),
        "r_k": nrm((L, N_HEADS, HEAD_DIM), 0.1),
        "lnx_g": gain((L, D_RWKV)), "lnx_b": nrm((L, D_RWKV), 0.02),
        "w_out": nrm((L, D_MODEL, D_MODEL), BETA * D_MODEL ** -0.5),
        "ple_gate_w": nrm((L, D_MODEL, D_MODEL), D_MODEL ** -0.5),
        "ple_gate_b": nrm((L, D_MODEL), 0.02),
        "ple_w": nrm((L, D_PLE, D_MODEL), BETA * D_PLE ** -0.5),
    }


def reference(x_prompt, x_sample, p_prompt, p_sample, state_pool, state_shift, state_wkv,
              ln1_g, ln1_b, ln2_g, ln2_b, ln3_g, ln3_b,
              ffn1_w_in, ffn1_w_out, ffn2_w_in, ffn2_w_out,
              w_in, mu_shift, pool_w, pool_scale, w0, w2, a0, a2, g2, k_k, k_a, r_k,
              lnx_g, lnx_b, w_out, ple_gate_w, ple_gate_b, ple_w):
    W = dict(ln1_g=ln1_g, ln1_b=ln1_b, ln2_g=ln2_g, ln2_b=ln2_b, ln3_g=ln3_g, ln3_b=ln3_b,
             ffn1_w_in=ffn1_w_in, ffn1_w_out=ffn1_w_out, ffn2_w_in=ffn2_w_in, ffn2_w_out=ffn2_w_out,
             w_in=w_in, mu_shift=mu_shift, pool_w=pool_w, pool_scale=pool_scale, w0=w0, w2=w2,
             a0=a0, a2=a2, g2=g2, k_k=k_k, k_a=k_a, r_k=r_k, lnx_g=lnx_g, lnx_b=lnx_b,
             w_out=w_out, ple_gate_w=ple_gate_w, ple_gate_b=ple_gate_b, ple_w=ple_w)
    B = x_prompt.shape[0]
    dt = x_prompt.dtype
    y_prompt, pool_prompt, shift_prompt, wkv_prompt = run_trunk(
        x_prompt, p_prompt, 0,
        jnp.zeros((DEPTH, B, 0, D_POOL), dt),
        jnp.zeros((DEPTH, B, D_SHIFT), dt),
        jnp.zeros((DEPTH, B, N_HEADS, HEAD_DIM, HEAD_DIM), dt), W)
    y_sample, pool_sample, shift_sample, wkv_sample = run_trunk(
        x_sample, p_sample, PAST_LEN, state_pool, state_shift, state_wkv, W)
    return (y_prompt, y_sample, pool_prompt, shift_prompt, wkv_prompt, pool_sample, shift_sample, wkv_sample)
```

```python
import functools
import math

import jax
import jax.numpy as jnp
from jax import lax
from jax.experimental import pallas as pl
from jax.experimental.pallas import tpu as pltpu

D_MODEL = 1024
D_POOL = 256
D_RWKV = 768
HEAD_DIM = 64
N_HEADS = 12
LANES = 128
N_PAIRS = D_RWKV // LANES
D_SHIFT = 2560
D_FF = 2816
D_PLE = 256
POOL_WINDOWS = (2, 4, 8, 16)
POOL_HIST = 15
POOL_PAD = 16
DEPTH = 2
ALPHA = (2.0 * DEPTH) ** 0.25
LN_EPS = 1e-5
LNX_EPS = 64e-5
PAST_LEN = 1024
CHUNK = 64
TOKEN_TILE = 256
V7X_VMEM_BYTES = 64 * 1024 * 1024

F32 = jnp.float32
BF16 = jnp.bfloat16
_NT = (((1,), (1,)), ((), ()))
_TN = (((0,), (0,)), ((), ()))


def _dot(a, b):
    return jnp.dot(a, b, preferred_element_type=F32)


def _sigmoid(x):
    return 1.0 / (1.0 + jnp.exp(-x))


def _layer_norm(h, g, b, eps):
    mu = jnp.mean(h, axis=-1, keepdims=True)
    d = h - mu
    var = jnp.mean(d * d, axis=-1, keepdims=True)
    return d * lax.rsqrt(var + eps) * g + b


def _swiglu(xb, w_in_ref, w_out_ref):
    gate = _dot(xb, w_in_ref[:, :D_FF])
    up = _dot(xb, w_in_ref[:, D_FF:])
    act = (gate * _sigmoid(gate) * up).astype(BF16)
    return _dot(act, w_out_ref[...])


def _ffn1_kernel(x_ref, w_in_ref, w_out_ref, g_ref, b_ref, wz_ref, x1_ref, u_ref, zr_ref):
    x = x_ref[...]
    h = ALPHA * x + 0.5 * _swiglu(x.astype(BF16), w_in_ref, w_out_ref)
    x1 = _layer_norm(h, g_ref[...], b_ref[...], LN_EPS)
    x1_ref[...] = x1
    z = _dot(x1.astype(BF16), wz_ref[...])
    u_ref[...] = z[:, :D_POOL]
    zr_ref[...] = z[:, D_POOL:]


def _post_kernel(mix_ref, x1_ref, p_ref, wo_ref, g2_ref, b2_ref, w_in_ref, w_out_ref,
                 wg_ref, bg_ref, wp_ref, g3_ref, b3_ref, out_ref):
    mix = _dot(mix_ref[...], wo_ref[...])
    x2 = _layer_norm(ALPHA * x1_ref[...] + mix, g2_ref[...], b2_ref[...], LN_EPS)
    h = ALPHA * x2 + 0.5 * _swiglu(x2.astype(BF16), w_in_ref, w_out_ref)
    gate = _sigmoid(_dot(h.astype(BF16), wg_ref[...]) + bg_ref[...])
    h = h + gate * _dot(p_ref[...].astype(BF16), wp_ref[...])
    out_ref[...] = _layer_norm(h, g3_ref[...], b3_ref[...], LN_EPS)


def _mix_kernel(t_valid, pos0,
                u_ref, zr_ref, hist_ref, shift_ref, s0_ref,
                mu_ref, w0_ref, w2_ref, a0_ref, a2_ref, g2_ref, kk_ref, ka_ref, rk_ref,
                lng_ref, lnb_ref, pw_ref, ps_ref,
                out_ref, s_ref, uext_ref, prev_ref):
    L = CHUNK
    c = pl.program_id(1)

    @pl.when(c == 0)
    def _():
        uext_ref[0:POOL_PAD, :] = hist_ref[0]
        prev_ref[...] = shift_ref[0]
        s_ref[0] = s0_ref[0]

    row = lax.broadcasted_iota(jnp.int32, (L, 1), 0)
    lane = lax.broadcasted_iota(jnp.int32, (1, LANES), 1)
    left = lane < HEAD_DIM
    col = lane & (HEAD_DIM - 1)

    u = u_ref[0]
    uext_ref[POOL_PAD:POOL_PAD + L, :] = u
    run = u
    sums = {}
    for j in range(1, max(POOL_WINDOWS)):
        run = run + uext_ref[POOL_PAD - j:POOL_PAD - j + L, :]
        if j + 1 in POOL_WINDOWS:
            sums[j + 1] = run
    uext_ref[0:POOL_PAD, :] = uext_ref[L:L + POOL_PAD, :]
    pos1 = pos0 + c * L + row + 1
    lane_u = lax.broadcasted_iota(jnp.int32, (1, D_POOL), 1)
    means = sums[16] / jnp.minimum(pos1, 16).astype(F32)
    for gi in (2, 1, 0):
        w = POOL_WINDOWS[gi]
        means = jnp.where(lane_u < (gi + 1) * HEAD_DIM, sums[w] / jnp.minimum(pos1, w).astype(F32), means)
    pooled = means - u
    pool_out = _dot(pooled.astype(BF16), pw_ref[...]) * ps_ref[...]

    zr = zr_ref[0]
    zprev = jnp.where(row == 0, prev_ref[...], pltpu.roll(zr, 1, 0))
    prev_ref[...] = zr_ref[0, pl.ds(L - 1, 1), :]
    zs = zr + (zprev - zr) * mu_ref[...]
    r = zs[:, 0:D_RWKV]
    k = zs[:, D_RWKV:2 * D_RWKV]
    v = zs[:, 2 * D_RWKV:3 * D_RWKV]
    lwa = zs[:, 3 * D_RWKV:3 * D_RWKV + LANES]
    lg = zs[:, 3 * D_RWKV + LANES:]
    tl = jnp.where(left, jnp.tanh(lwa), lwa).astype(BF16)
    logw = -math.exp(-0.5) * _sigmoid(w0_ref[...] + _dot(tl, w2_ref[...]))
    a = _sigmoid(a0_ref[...] + _dot(tl, a2_ref[...]))
    g = _dot(_sigmoid(lg).astype(BF16), g2_ref[...])

    ones_bd = ((lax.broadcasted_iota(jnp.int32, (LANES, LANES), 0) < HEAD_DIM)
               == (lax.broadcasted_iota(jnp.int32, (LANES, LANES), 1) < HEAD_DIM)).astype(BF16)

    def head_sum(x):
        return jnp.concatenate(
            [_dot(x[:, p * LANES:(p + 1) * LANES].astype(BF16), ones_bd) for p in range(N_PAIRS)], axis=1)

    kkr = k * kk_ref[...]
    kk = kkr / jnp.maximum(jnp.sqrt(head_sum(kkr * kkr)), 1e-12)
    k2 = k * (1.0 + (a - 1.0) * ka_ref[...])
    b = kk * a
    if t_valid < L:
        valid = row < t_valid
        logw = jnp.where(valid, logw, 0.0)
        b = jnp.where(valid, b, 0.0)
        k2 = jnp.where(valid, k2, 0.0)

    tri = (lax.broadcasted_iota(jnp.int32, (L, L), 0) >= lax.broadcasted_iota(jnp.int32, (L, L), 1)).astype(BF16)
    hi = logw.astype(BF16)
    rem = logw - hi.astype(F32)
    mid = rem.astype(BF16)
    lo = (rem - mid.astype(F32)).astype(BF16)
    cum = _dot(tri, hi) + _dot(tri, mid) + _dot(tri, lo)
    cum_l = jnp.sum(jnp.where(row == L - 1, cum, 0.0), axis=0, keepdims=True)
    e_inv = jnp.exp(-cum)
    e_tail = jnp.exp(cum_l - cum)
    p_l = jnp.exp(cum_l)
    at = (-kk * jnp.exp(cum - logw)).astype(BF16)
    rt = (r * jnp.exp(cum)).astype(BF16)
    bt = (b * e_inv).astype(BF16)
    kt = (k2 * e_inv).astype(BF16)
    bh = (b * e_tail).astype(BF16)
    kh = (k2 * e_tail).astype(BF16)
    vb = v.astype(BF16)

    strict = col < row
    incl = col <= row
    eye = (col == row).astype(F32)

    def bd(x):
        z = jnp.zeros_like(x)
        return jnp.concatenate([jnp.where(left, x, z), jnp.where(left, z, x)], axis=0)

    def pmm(a_, x_):
        return _dot(a_.astype(BF16), bd(x_.astype(BF16)))

    ys = []
    for p in range(N_PAIRS):
        sl = slice(p * LANES, (p + 1) * LANES)
        s_p = s_ref[0, :, sl]
        ar = jnp.concatenate([at[:, sl], rt[:, sl]], axis=0)
        zall = jnp.concatenate([bd(bt[:, sl]), bd(kt[:, sl]), bd(s_p.astype(BF16))], axis=0)
        big = lax.dot_general(ar, zall, _NT, preferred_element_type=F32)
        n_ = jnp.where(strict, big[0:L, 0:2 * L], 0.0)
        wb = jnp.where(incl, big[L:2 * L, 0:2 * L], 0.0)
        m_ = jnp.where(strict, big[0:L, 2 * L:4 * L], 0.0)
        wk = jnp.where(incl, big[L:2 * L, 2 * L:4 * L], 0.0)
        a_s = big[0:L, 4 * L:]
        r_s = big[L:2 * L, 4 * L:]
        tm = eye + n_
        pw = n_
        for _ in range(int(math.log2(L)) - 1):
            pw = pmm(pw, pw)
            tm = tm + pmm(tm, pw)
        v_p = vb[:, sl]
        u_ = pmm(tm, a_s + pmm(m_, v_p))
        u_b = u_.astype(BF16)
        y = r_s + _dot(jnp.concatenate([wb, wk], axis=1).astype(BF16),
                       jnp.concatenate([bd(u_b), bd(v_p)], axis=0))
        full = lax.dot_general(jnp.concatenate([u_b, v_p], axis=0),
                               jnp.concatenate([bh[:, sl], kh[:, sl]], axis=0),
                               _TN, preferred_element_type=F32)
        s_ref[0, :, sl] = s_p * p_l[:, sl] + jnp.where(left, full[0:HEAD_DIM], full[HEAD_DIM:])
        ys.append(y)
    y = jnp.concatenate(ys, axis=1)

    mean = head_sum(y) * (1.0 / HEAD_DIM)
    yc = y - mean
    var = head_sum(yc * yc) * (1.0 / HEAD_DIM)
    yn = yc * lax.rsqrt(var + LNX_EPS) * lng_ref[...] + lnb_ref[...]
    bonus = head_sum(r * k2 * rk_ref[...])
    rwkv_out = (yn + bonus * v) * g
    out_ref[0] = jnp.concatenate([pool_out, rwkv_out], axis=1).astype(BF16)


def _const_spec(shape):
    return pl.BlockSpec(shape, lambda *_: (0,) * len(shape), pipeline_mode=pl.Buffered(1))


def _row_spec(width):
    return pl.BlockSpec((TOKEN_TILE, width), lambda i: (i, 0))


def _ffn1_call(x, lw):
    n = x.shape[0]
    consts = (lw["ffn1_w_in"], lw["ffn1_w_out"], lw["ln1_g"], lw["ln1_b"], lw["w_in"])
    return pl.pallas_call(
        _ffn1_kernel,
        grid=(n // TOKEN_TILE,),
        in_specs=[_row_spec(D_MODEL)] + [_const_spec(c.shape) for c in consts],
        out_specs=[_row_spec(D_MODEL), _row_spec(D_POOL), _row_spec(D_SHIFT)],
        out_shape=[jax.ShapeDtypeStruct((n, D_MODEL), F32),
                   jax.ShapeDtypeStruct((n, D_POOL), F32),
                   jax.ShapeDtypeStruct((n, D_SHIFT), F32)],
        compiler_params=pltpu.CompilerParams(dimension_semantics=("parallel",),
                                             vmem_limit_bytes=V7X_VMEM_BYTES * 7 // 8),
        name="ffn1",
    )(x, *consts)


def _post_call(mix, x1, p, lw):
    n = x1.shape[0]
    consts = (lw["w_out"], lw["ln2_g"], lw["ln2_b"], lw["ffn2_w_in"], lw["ffn2_w_out"],
              lw["ple_gate_w"], lw["ple_gate_b"], lw["ple_w"], lw["ln3_g"], lw["ln3_b"])
    return pl.pallas_call(
        _post_kernel,
        grid=(n // TOKEN_TILE,),
        in_specs=[_row_spec(D_MODEL), _row_spec(D_MODEL), _row_spec(D_PLE)]
        + [_const_spec(c.shape) for c in consts],
        out_specs=_row_spec(D_MODEL),
        out_shape=jax.ShapeDtypeStruct((n, D_MODEL), F32),
        compiler_params=pltpu.CompilerParams(dimension_semantics=("parallel",),
                                             vmem_limit_bytes=V7X_VMEM_BYTES * 7 // 8),
        name="post",
    )(mix, x1, p, *consts)


def _mix_call(u, zr, hist, shift, s0, lw, t_valid, pos0):
    bn, tp, _ = u.shape
    consts = (lw["mu_shift"], lw["w0"], lw["w2"], lw["a0"], lw["a2"], lw["g2"], lw["k_k"], lw["k_a"],
              lw["r_k"], lw["lnx_g"], lw["lnx_b"], lw["pool_w"], lw["pool_scale"])
    per_row = lambda shape: pl.BlockSpec((1,) + shape, lambda bi, ci: (bi, 0, 0))
    per_chunk = lambda width: pl.BlockSpec((1, CHUNK, width), lambda bi, ci: (bi, ci, 0))
    return pl.pallas_call(
        functools.partial(_mix_kernel, t_valid, pos0),
        grid=(bn, tp // CHUNK),
        in_specs=[per_chunk(D_POOL), per_chunk(D_SHIFT), per_row((POOL_PAD, D_POOL)),
                  per_row((1, D_SHIFT)), per_row((HEAD_DIM, D_RWKV))]
        + [_const_spec(c.shape) for c in consts],
        out_specs=[per_chunk(D_MODEL), per_row((HEAD_DIM, D_RWKV))],
        out_shape=[jax.ShapeDtypeStruct((bn, tp, D_MODEL), BF16),
                   jax.ShapeDtypeStruct((bn, HEAD_DIM, D_RWKV), F32)],
        scratch_shapes=[pltpu.VMEM((POOL_PAD + CHUNK, D_POOL), F32), pltpu.VMEM((1, D_SHIFT), F32)],
        compiler_params=pltpu.CompilerParams(dimension_semantics=("parallel", "arbitrary")),
        name="mix",
    )(u, zr, hist, shift, s0, *consts)


def _prepare_layer(i, W):
    row = lambda name: W[name][i].reshape(1, -1).astype(F32)
    mat = lambda name: W[name][i].astype(BF16)
    zeros = jnp.zeros((HEAD_DIM, D_RWKV), BF16)
    lw = {name: mat(name) for name in ("ffn1_w_in", "ffn1_w_out", "ffn2_w_in", "ffn2_w_out", "w_in", "g2",
                                       "w_out", "ple_gate_w", "ple_w")}
    lw.update({name: row(name) for name in ("ln1_g", "ln1_b", "ln2_g", "ln2_b", "ln3_g", "ln3_b", "mu_shift",
                                            "pool_scale", "w0", "a0", "k_k", "k_a", "r_k", "lnx_g", "lnx_b",
                                            "ple_gate_b")})
    lw["w2"] = jnp.concatenate([mat("w2"), zeros], axis=0)
    lw["a2"] = jnp.concatenate([zeros, mat("a2")], axis=0)
    lw["pool_w"] = jax.scipy.linalg.block_diag(*[W["pool_w"][i, gi] for gi in range(len(POOL_WINDOWS))]).astype(BF16)
    return lw


def _run_trunk(x, p, pos0, pool_hist, shift_prev, wkv_prev, layers):
    bn, t, _ = x.shape
    n = bn * t
    tp = -(-t // CHUNK) * CHUNK
    xf = x.reshape(n, D_MODEL)
    pools, shifts, wkvs = [], [], []
    for i, lw in enumerate(layers):
        x1, u, zr = _ffn1_call(xf, lw)
        u = u.reshape(bn, t, D_POOL)
        zr = zr.reshape(bn, t, D_SHIFT)
        hist = jnp.zeros((bn, POOL_HIST, D_POOL), F32) if pool_hist is None else pool_hist[i]
        pools.append(jnp.concatenate([hist, u], axis=1)[:, -POOL_HIST:])
        shifts.append(zr[:, -1])
        if tp != t:
            u = jnp.pad(u, ((0, 0), (0, tp - t), (0, 0)))
            zr = jnp.pad(zr, ((0, 0), (0, tp - t), (0, 0)))
        hist = jnp.pad(hist, ((0, 0), (POOL_PAD - POOL_HIST, 0), (0, 0)))
        s0 = jnp.transpose(wkv_prev[i], (0, 2, 1, 3)).reshape(bn, HEAD_DIM, D_RWKV)
        mix, s_new = _mix_call(u, zr, hist, shift_prev[i][:, None, :], s0, lw, min(t, CHUNK) if tp != t else CHUNK,
                               pos0)
        wkvs.append(jnp.transpose(s_new.reshape(bn, HEAD_DIM, N_HEADS, HEAD_DIM), (0, 2, 1, 3)))
        xf = _post_call(mix[:, :t].reshape(n, D_MODEL), x1, p[i].reshape(n, D_PLE), lw)
    return xf.reshape(bn, t, D_MODEL), jnp.stack(pools), jnp.stack(shifts), jnp.stack(wkvs)


def kernel(x_prompt, x_sample, p_prompt, p_sample, state_pool, state_shift, state_wkv, ln1_g, ln1_b, ln2_g, ln2_b, ln3_g, ln3_b, ffn1_w_in, ffn1_w_out, ffn2_w_in, ffn2_w_out, w_in, mu_shift, pool_w, pool_scale, w0, w2, a0, a2, g2, k_k, k_a, r_k, lnx_g, lnx_b, w_out, ple_gate_w, ple_gate_b, ple_w):
    W = dict(ln1_g=ln1_g, ln1_b=ln1_b, ln2_g=ln2_g, ln2_b=ln2_b, ln3_g=ln3_g, ln3_b=ln3_b,
             ffn1_w_in=ffn1_w_in, ffn1_w_out=ffn1_w_out, ffn2_w_in=ffn2_w_in, ffn2_w_out=ffn2_w_out,
             w_in=w_in, mu_shift=mu_shift, pool_w=pool_w, pool_scale=pool_scale, w0=w0, w2=w2,
             a0=a0, a2=a2, g2=g2, k_k=k_k, k_a=k_a, r_k=r_k, lnx_g=lnx_g, lnx_b=lnx_b,
             w_out=w_out, ple_gate_w=ple_gate_w, ple_gate_b=ple_gate_b, ple_w=ple_w)
    layers = [_prepare_layer(i, W) for i in range(DEPTH)]
    bn = x_prompt.shape[0]
    y_prompt, pool_prompt, shift_prompt, wkv_prompt = _run_trunk(
        x_prompt, p_prompt, 0, None,
        jnp.zeros((DEPTH, bn, D_SHIFT), F32),
        jnp.zeros((DEPTH, bn, N_HEADS, HEAD_DIM, HEAD_DIM), F32), layers)
    y_sample, pool_sample, shift_sample, wkv_sample = _run_trunk(
        x_sample, p_sample, PAST_LEN, state_pool, state_shift, state_wkv, layers)
    return (y_prompt, y_sample, pool_prompt, shift_prompt, wkv_prompt, pool_sample, shift_sample, wkv_sample)
```

```python
import functools
import math

import jax
import jax.numpy as jnp
from jax import lax
from jax.experimental import pallas as pl
from jax.experimental.pallas import tpu as pltpu

D_MODEL = 1024
D_POOL = 256
D_RWKV = 768
HEAD_DIM = 64
N_HEADS = 12
LANES = 128
N_PAIRS = D_RWKV // LANES
D_SHIFT = 2560
D_FF = 2816
D_PLE = 256
POOL_WINDOWS = (2, 4, 8, 16)
POOL_HIST = 15
POOL_PAD = 16
DEPTH = 2
ALPHA = (2.0 * DEPTH) ** 0.25
LN_EPS = 1e-5
LNX_EPS = 64e-5
PAST_LEN = 1024
CHUNK = 64
TOKEN_TILE = 256
V7X_VMEM_BYTES = 64 * 1024 * 1024

F32 = jnp.float32
BF16 = jnp.bfloat16
_NT = (((1,), (1,)), ((), ()))
_TN = (((0,), (0,)), ((), ()))


def _dot(a, b):
    return jnp.dot(a, b, preferred_element_type=F32)


def _sigmoid(x):
    return 0.5 * jnp.tanh(0.5 * x) + 0.5


def _layer_norm(h, g, b, eps):
    mu = jnp.mean(h, axis=-1, keepdims=True)
    d = h - mu
    var = jnp.mean(d * d, axis=-1, keepdims=True)
    return d * lax.rsqrt(var + eps) * g + b


def _swiglu(xb, w_in_ref, w_out_ref):
    gate = _dot(xb, w_in_ref[:, :D_FF])
    up = _dot(xb, w_in_ref[:, D_FF:])
    act = (gate * _sigmoid(gate) * up).astype(BF16)
    return _dot(act, w_out_ref[...])


def _ffn1_kernel(x_ref, w_in_ref, w_out_ref, g_ref, b_ref, wz_ref, x1_ref, u_ref, zr_ref):
    x = x_ref[...]
    h = ALPHA * x + 0.5 * _swiglu(x.astype(BF16), w_in_ref, w_out_ref)
    x1 = _layer_norm(h, g_ref[...], b_ref[...], LN_EPS)
    x1_ref[...] = x1
    z = _dot(x1.astype(BF16), wz_ref[...])
    u_ref[...] = z[:, :D_POOL]
    zr_ref[...] = z[:, D_POOL:]


def _post_kernel(mix_ref, x1_ref, p_ref, wo_ref, g2_ref, b2_ref, w_in_ref, w_out_ref,
                 wg_ref, bg_ref, wp_ref, g3_ref, b3_ref, out_ref):
    mix = _dot(mix_ref[...], wo_ref[...])
    x2 = _layer_norm(ALPHA * x1_ref[...] + mix, g2_ref[...], b2_ref[...], LN_EPS)
    h = ALPHA * x2 + 0.5 * _swiglu(x2.astype(BF16), w_in_ref, w_out_ref)
    gate = _sigmoid(_dot(h.astype(BF16), wg_ref[...]) + bg_ref[...])
    h = h + gate * _dot(p_ref[...].astype(BF16), wp_ref[...])
    out_ref[...] = _layer_norm(h, g3_ref[...], b3_ref[...], LN_EPS)


def _mix_kernel(t_valid, pos0,
                u_ref, zr_ref, hist_ref, shift_ref, s0_ref,
                mu_ref, w0_ref, w2_ref, a0_ref, a2_ref, g2_ref, kk_ref, ka_ref, rk_ref,
                lng_ref, lnb_ref, pw_ref, ps_ref,
                out_ref, s_ref, uext_ref, prev_ref):
    L = CHUNK
    c = pl.program_id(1)

    @pl.when(c == 0)
    def _():
        uext_ref[0:POOL_PAD, :] = hist_ref[0]
        prev_ref[...] = shift_ref[0]
        s_ref[0] = s0_ref[0]

    row = lax.broadcasted_iota(jnp.int32, (L, 1), 0)
    lane = lax.broadcasted_iota(jnp.int32, (1, LANES), 1)
    left = lane < HEAD_DIM
    col = lane & (HEAD_DIM - 1)

    u = u_ref[0]
    uext_ref[POOL_PAD:POOL_PAD + L, :] = u
    run = u
    sums = {}
    for j in range(1, max(POOL_WINDOWS)):
        run = run + uext_ref[POOL_PAD - j:POOL_PAD - j + L, :]
        if j + 1 in POOL_WINDOWS:
            sums[j + 1] = run
    uext_ref[0:POOL_PAD, :] = uext_ref[L:L + POOL_PAD, :]
    pos1 = pos0 + c * L + row + 1
    lane_u = lax.broadcasted_iota(jnp.int32, (1, D_POOL), 1)
    means = sums[16] / jnp.minimum(pos1, 16).astype(F32)
    for gi in (2, 1, 0):
        w = POOL_WINDOWS[gi]
        means = jnp.where(lane_u < (gi + 1) * HEAD_DIM, sums[w] / jnp.minimum(pos1, w).astype(F32), means)
    pooled = means - u
    pool_out = _dot(pooled.astype(BF16), pw_ref[...]) * ps_ref[...]

    zr = zr_ref[0]
    zprev = jnp.where(row == 0, prev_ref[...], pltpu.roll(zr, 1, 0))
    prev_ref[...] = zr_ref[0, pl.ds(L - 1, 1), :]
    zs = zr + (zprev - zr) * mu_ref[...]
    r = zs[:, 0:D_RWKV]
    k = zs[:, D_RWKV:2 * D_RWKV]
    v = zs[:, 2 * D_RWKV:3 * D_RWKV]
    lwa = zs[:, 3 * D_RWKV:3 * D_RWKV + LANES]
    lg = zs[:, 3 * D_RWKV + LANES:]
    tl = jnp.where(left, jnp.tanh(lwa), lwa).astype(BF16)
    logw = -math.exp(-0.5) * _sigmoid(w0_ref[...] + _dot(tl, w2_ref[...]))
    a = _sigmoid(a0_ref[...] + _dot(tl, a2_ref[...]))
    g = _dot(_sigmoid(lg).astype(BF16), g2_ref[...])

    ones_bd = ((lax.broadcasted_iota(jnp.int32, (LANES, LANES), 0) < HEAD_DIM)
               == (lax.broadcasted_iota(jnp.int32, (LANES, LANES), 1) < HEAD_DIM)).astype(BF16)

    def head_sum(x):
        tiles = jnp.concatenate([x[:, p * LANES:(p + 1) * LANES] for p in range(N_PAIRS)], axis=0)
        sums = _dot(tiles.astype(BF16), ones_bd)
        return jnp.concatenate([sums[p * L:(p + 1) * L] for p in range(N_PAIRS)], axis=1)

    kkr = k * kk_ref[...]
    kk = kkr * jnp.minimum(lax.rsqrt(head_sum(kkr * kkr)), 1e12)
    k2 = k * (1.0 + (a - 1.0) * ka_ref[...])
    b = kk * a
    if t_valid < L:
        valid = row < t_valid
        logw = jnp.where(valid, logw, 0.0)
        b = jnp.where(valid, b, 0.0)
        k2 = jnp.where(valid, k2, 0.0)

    tri = (lax.broadcasted_iota(jnp.int32, (L, L), 0) >= lax.broadcasted_iota(jnp.int32, (L, L), 1)).astype(BF16)
    hi = logw.astype(BF16)
    lo = (logw - hi.astype(F32)).astype(BF16)
    cum = _dot(tri, hi) + _dot(tri, lo)
    cum_l = jnp.sum(jnp.where(row == L - 1, cum, 0.0), axis=0, keepdims=True)
    e_inv = jnp.exp(-cum)
    e_tail = jnp.exp(cum_l - cum)
    p_l = jnp.exp(cum_l)
    at = (-kk * jnp.exp(cum - logw)).astype(BF16)
    rt = (r * jnp.exp(cum)).astype(BF16)
    bt = (b * e_inv).astype(BF16)
    kt = (k2 * e_inv).astype(BF16)
    bh = (b * e_tail).astype(BF16)
    kh = (k2 * e_tail).astype(BF16)
    vb = v.astype(BF16)

    strict = col < row
    incl = col <= row
    eye = (col == row).astype(F32)

    def bd(x):
        z = jnp.zeros_like(x)
        return jnp.concatenate([jnp.where(left, x, z), jnp.where(left, z, x)], axis=0)

    def pmm(a_, x_):
        return _dot(a_.astype(BF16), bd(x_.astype(BF16)))

    pairs = range(N_PAIRS)
    sls = [slice(p * LANES, (p + 1) * LANES) for p in pairs]
    s_all = s_ref[0]
    s_old = [s_all[:, sl] for sl in sls]
    v_p = [vb[:, sl] for sl in sls]
    big = [lax.dot_general(jnp.concatenate([at[:, sl], rt[:, sl]], axis=0),
                           jnp.concatenate([bd(bt[:, sl]), bd(kt[:, sl]), bd(s_old[p].astype(BF16))], axis=0),
                           _NT, preferred_element_type=F32) for p, sl in enumerate(sls)]
    n_ = [jnp.where(strict, x[0:L, 0:2 * L], 0.0) for x in big]
    wb = [jnp.where(incl, x[L:2 * L, 0:2 * L], 0.0) for x in big]
    m_ = [jnp.where(strict, x[0:L, 2 * L:4 * L], 0.0) for x in big]
    wk = [jnp.where(incl, x[L:2 * L, 2 * L:4 * L], 0.0) for x in big]
    bdv = [bd(x) for x in v_p]
    mwv = [_dot(jnp.concatenate([m_[p], wk[p]], axis=0).astype(BF16), bdv[p]) for p in pairs]
    tm = [eye + x for x in n_]
    pw = [pmm(x, x) for x in n_]
    n_stages = int(math.log2(L))
    for stage in range(2, n_stages + 1):
        last = stage == n_stages
        bdp = [bd(x.astype(BF16)) for x in pw]
        prod = [_dot((tm[p] if last else jnp.concatenate([tm[p], pw[p]], axis=0)).astype(BF16), bdp[p])
                for p in pairs]
        tm = [tm[p] + prod[p][0:L] for p in pairs]
        if not last:
            pw = [x[L:2 * L] for x in prod]
    u_b = [pmm(tm[p], big[p][0:L, 4 * L:] + mwv[p][0:L]).astype(BF16) for p in pairs]
    ys = [big[p][L:2 * L, 4 * L:] + mwv[p][L:2 * L] + pmm(wb[p], u_b[p]) for p in pairs]
    full = [lax.dot_general(jnp.concatenate([u_b[p], v_p[p]], axis=0),
                            jnp.concatenate([bh[:, sl], kh[:, sl]], axis=0),
                            _TN, preferred_element_type=F32) for p, sl in enumerate(sls)]
    s_ref[0] = jnp.concatenate(
        [s_old[p] * p_l[:, sl] + jnp.where(left, full[p][0:HEAD_DIM], full[p][HEAD_DIM:])
         for p, sl in enumerate(sls)], axis=1)
    y = jnp.concatenate(ys, axis=1)

    mean = head_sum(y) * (1.0 / HEAD_DIM)
    yc = y - mean
    var = head_sum(yc * yc) * (1.0 / HEAD_DIM)
    yn = yc * lax.rsqrt(var + LNX_EPS) * lng_ref[...] + lnb_ref[...]
    bonus = head_sum(r * k2 * rk_ref[...])
    rwkv_out = (yn + bonus * v) * g
    out_ref[0] = jnp.concatenate([pool_out, rwkv_out], axis=1).astype(BF16)


def _const_spec(shape):
    return pl.BlockSpec(shape, lambda *_: (0,) * len(shape), pipeline_mode=pl.Buffered(1))


def _row_spec(width):
    return pl.BlockSpec((TOKEN_TILE, width), lambda i: (i, 0))


def _ffn1_call(x, lw):
    n = x.shape[0]
    consts = (lw["ffn1_w_in"], lw["ffn1_w_out"], lw["ln1_g"], lw["ln1_b"], lw["w_in"])
    return pl.pallas_call(
        _ffn1_kernel,
        grid=(n // TOKEN_TILE,),
        in_specs=[_row_spec(D_MODEL)] + [_const_spec(c.shape) for c in consts],
        out_specs=[_row_spec(D_MODEL), _row_spec(D_POOL), _row_spec(D_SHIFT)],
        out_shape=[jax.ShapeDtypeStruct((n, D_MODEL), F32),
                   jax.ShapeDtypeStruct((n, D_POOL), F32),
                   jax.ShapeDtypeStruct((n, D_SHIFT), F32)],
        compiler_params=pltpu.CompilerParams(dimension_semantics=("parallel",),
                                             vmem_limit_bytes=V7X_VMEM_BYTES * 7 // 8),
        name="ffn1",
    )(x, *consts)


def _post_call(mix, x1, p, lw):
    n = x1.shape[0]
    consts = (lw["w_out"], lw["ln2_g"], lw["ln2_b"], lw["ffn2_w_in"], lw["ffn2_w_out"],
              lw["ple_gate_w"], lw["ple_gate_b"], lw["ple_w"], lw["ln3_g"], lw["ln3_b"])
    return pl.pallas_call(
        _post_kernel,
        grid=(n // TOKEN_TILE,),
        in_specs=[_row_spec(D_MODEL), _row_spec(D_MODEL), _row_spec(D_PLE)]
        + [_const_spec(c.shape) for c in consts],
        out_specs=_row_spec(D_MODEL),
        out_shape=jax.ShapeDtypeStruct((n, D_MODEL), F32),
        compiler_params=pltpu.CompilerParams(dimension_semantics=("parallel",),
                                             vmem_limit_bytes=V7X_VMEM_BYTES * 7 // 8),
        name="post",
    )(mix, x1, p, *consts)


def _mix_call(u, zr, hist, shift, s0, lw, t_valid, pos0):
    bn, tp, _ = u.shape
    consts = (lw["mu_shift"], lw["w0"], lw["w2"], lw["a0"], lw["a2"], lw["g2"], lw["k_k"], lw["k_a"],
              lw["r_k"], lw["lnx_g"], lw["lnx_b"], lw["pool_w"], lw["pool_scale"])
    per_row = lambda shape: pl.BlockSpec((1,) + shape, lambda bi, ci: (bi, 0, 0))
    per_chunk = lambda width: pl.BlockSpec((1, CHUNK, width), lambda bi, ci: (bi, ci, 0))
    return pl.pallas_call(
        functools.partial(_mix_kernel, t_valid, pos0),
        grid=(bn, tp // CHUNK),
        in_specs=[per_chunk(D_POOL), per_chunk(D_SHIFT), per_row((POOL_PAD, D_POOL)),
                  per_row((1, D_SHIFT)), per_row((HEAD_DIM, D_RWKV))]
        + [_const_spec(c.shape) for c in consts],
        out_specs=[per_chunk(D_MODEL), per_row((HEAD_DIM, D_RWKV))],
        out_shape=[jax.ShapeDtypeStruct((bn, tp, D_MODEL), BF16),
                   jax.ShapeDtypeStruct((bn, HEAD_DIM, D_RWKV), F32)],
        scratch_shapes=[pltpu.VMEM((POOL_PAD + CHUNK, D_POOL), F32), pltpu.VMEM((1, D_SHIFT), F32)],
        compiler_params=pltpu.CompilerParams(dimension_semantics=("parallel", "arbitrary")),
        name="mix",
    )(u, zr, hist, shift, s0, *consts)


def _prepare_layer(i, W):
    row = lambda name: W[name][i].reshape(1, -1).astype(F32)
    mat = lambda name: W[name][i].astype(BF16)
    zeros = jnp.zeros((HEAD_DIM, D_RWKV), BF16)
    lw = {name: mat(name) for name in ("ffn1_w_in", "ffn1_w_out", "ffn2_w_in", "ffn2_w_out", "w_in", "g2",
                                       "w_out", "ple_gate_w", "ple_w")}
    lw.update({name: row(name) for name in ("ln1_g", "ln1_b", "ln2_g", "ln2_b", "ln3_g", "ln3_b", "mu_shift",
                                            "pool_scale", "w0", "a0", "k_k", "k_a", "r_k", "lnx_g", "lnx_b",
                                            "ple_gate_b")})
    lw["w2"] = jnp.concatenate([mat("w2"), zeros], axis=0)
    lw["a2"] = jnp.concatenate([zeros, mat("a2")], axis=0)
    lw["pool_w"] = jax.scipy.linalg.block_diag(*[W["pool_w"][i, gi] for gi in range(len(POOL_WINDOWS))]).astype(BF16)
    return lw


def _run_trunk(x, p, pos0, pool_hist, shift_prev, wkv_prev, layers):
    bn, t, _ = x.shape
    n = bn * t
    tp = -(-t // CHUNK) * CHUNK
    xf = x.reshape(n, D_MODEL)
    pools, shifts, wkvs = [], [], []
    for i, lw in enumerate(layers):
        x1, u, zr = _ffn1_call(xf, lw)
        u = u.reshape(bn, t, D_POOL)
        zr = zr.reshape(bn, t, D_SHIFT)
        hist = jnp.zeros((bn, POOL_HIST, D_POOL), F32) if pool_hist is None else pool_hist[i]
        pools.append(jnp.concatenate([hist, u], axis=1)[:, -POOL_HIST:])
        shifts.append(zr[:, -1])
        if tp != t:
            u = jnp.pad(u, ((0, 0), (0, tp - t), (0, 0)))
            zr = jnp.pad(zr, ((0, 0), (0, tp - t), (0, 0)))
        hist = jnp.pad(hist, ((0, 0), (POOL_PAD - POOL_HIST, 0), (0, 0)))
        s0 = jnp.transpose(wkv_prev[i], (0, 2, 1, 3)).reshape(bn, HEAD_DIM, D_RWKV)
        mix, s_new = _mix_call(u, zr, hist, shift_prev[i][:, None, :], s0, lw, min(t, CHUNK) if tp != t else CHUNK,
                               pos0)
        wkvs.append(jnp.transpose(s_new.reshape(bn, HEAD_DIM, N_HEADS, HEAD_DIM), (0, 2, 1, 3)))
        xf = _post_call(mix[:, :t].reshape(n, D_MODEL), x1, p[i].reshape(n, D_PLE), lw)
    return xf.reshape(bn, t, D_MODEL), jnp.stack(pools), jnp.stack(shifts), jnp.stack(wkvs)


def kernel(x_prompt, x_sample, p_prompt, p_sample, state_pool, state_shift, state_wkv, ln1_g, ln1_b, ln2_g, ln2_b, ln3_g, ln3_b, ffn1_w_in, ffn1_w_out, ffn2_w_in, ffn2_w_out, w_in, mu_shift, pool_w, pool_scale, w0, w2, a0, a2, g2, k_k, k_a, r_k, lnx_g, lnx_b, w_out, ple_gate_w, ple_gate_b, ple_w):
    W = dict(ln1_g=ln1_g, ln1_b=ln1_b, ln2_g=ln2_g, ln2_b=ln2_b, ln3_g=ln3_g, ln3_b=ln3_b,
             ffn1_w_in=ffn1_w_in, ffn1_w_out=ffn1_w_out, ffn2_w_in=ffn2_w_in, ffn2_w_out=ffn2_w_out,
             w_in=w_in, mu_shift=mu_shift, pool_w=pool_w, pool_scale=pool_scale, w0=w0, w2=w2,
             a0=a0, a2=a2, g2=g2, k_k=k_k, k_a=k_a, r_k=r_k, lnx_g=lnx_g, lnx_b=lnx_b,
             w_out=w_out, ple_gate_w=ple_gate_w, ple_gate_b=ple_gate_b, ple_w=ple_w)
    layers = [_prepare_layer(i, W) for i in range(DEPTH)]
    bn = x_prompt.shape[0]
    y_prompt, pool_prompt, shift_prompt, wkv_prompt = _run_trunk(
        x_prompt, p_prompt, 0, None,
        jnp.zeros((DEPTH, bn, D_SHIFT), F32),
        jnp.zeros((DEPTH, bn, N_HEADS, HEAD_DIM, HEAD_DIM), F32), layers)
    y_sample, pool_sample, shift_sample, wkv_sample = _run_trunk(
        x_sample, p_sample, PAST_LEN, state_pool, state_shift, state_wkv, layers)
    return (y_prompt, y_sample, pool_prompt, shift_prompt, wkv_prompt, pool_sample, shift_sample, wkv_sample)
```

```python
import functools
import math

import jax
import jax.numpy as jnp
from jax import lax
from jax.experimental import pallas as pl
from jax.experimental.pallas import tpu as pltpu

D_MODEL = 1024
D_POOL = 256
D_RWKV = 768
HEAD_DIM = 64
N_HEADS = 12
LANES = 128
N_PAIRS = D_RWKV // LANES
D_SHIFT = 2560
D_FF = 2816
D_PLE = 256
POOL_WINDOWS = (2, 4, 8, 16)
POOL_HIST = 15
POOL_PAD = 16
DEPTH = 2
ALPHA = (2.0 * DEPTH) ** 0.25
LN_EPS = 1e-5
LNX_EPS = 64e-5
PAST_LEN = 1024
CHUNK = 64
TOKEN_TILE = 512
SUB_TILE = TOKEN_TILE // 2
F32_SUBLANES = 8
MIX_ROWS = 2
V7X_VMEM_BYTES = 64 * 1024 * 1024

F32 = jnp.float32
BF16 = jnp.bfloat16
_NT = (((1,), (1,)), ((), ()))
_TN = (((0,), (0,)), ((), ()))


def _dot(a, b):
    return jnp.dot(a, b, preferred_element_type=F32)


def _sigmoid(x):
    return 0.5 * jnp.tanh(0.5 * x) + 0.5


def _layer_norm(h, g, b, eps):
    mu = jnp.mean(h, axis=-1, keepdims=True)
    d = h - mu
    var = jnp.mean(d * d, axis=-1, keepdims=True)
    return d * lax.rsqrt(var + eps) * g + b


def _swiglu_halves(xbs, w_in_ref, w_out_ref):
    hs = [(_dot(xb, w_in_ref[:, :D_FF]), _dot(xb, w_in_ref[:, D_FF:])) for xb in xbs]
    return [_dot((gate * _sigmoid(gate) * up).astype(BF16), w_out_ref[...]) for gate, up in hs]


def _halves(ref):
    return [ref[0:SUB_TILE, :], ref[SUB_TILE:TOKEN_TILE, :]]


def _ffn1_kernel(x_ref, w_in_ref, w_out_ref, g_ref, b_ref, wz_ref, x1_ref, u_ref, zr_ref):
    xs = _halves(x_ref)
    ffn = _swiglu_halves([x.astype(BF16) for x in xs], w_in_ref, w_out_ref)
    for i, (x, f) in enumerate(zip(xs, ffn)):
        rows = slice(i * SUB_TILE, (i + 1) * SUB_TILE)
        x1 = _layer_norm(ALPHA * x + 0.5 * f, g_ref[...], b_ref[...], LN_EPS)
        x1_ref[rows, :] = x1
        z = _dot(x1.astype(BF16), wz_ref[...]).astype(BF16)
        u_ref[rows, :] = z[:, :D_POOL]
        zr_ref[rows, :] = z[:, D_POOL:]


def _post_kernel(mix_ref, x1_ref, p_ref, wo_ref, g2_ref, b2_ref, w_in_ref, w_out_ref,
                 wg_ref, bg_ref, wp_ref, g3_ref, b3_ref, out_ref):
    mixes = [_dot(m, wo_ref[...]) for m in _halves(mix_ref)]
    x2s = [_layer_norm(ALPHA * x1 + mix, g2_ref[...], b2_ref[...], LN_EPS) for x1, mix in zip(_halves(x1_ref), mixes)]
    ffn = _swiglu_halves([x2.astype(BF16) for x2 in x2s], w_in_ref, w_out_ref)
    ples = [_dot(p.astype(BF16), wp_ref[...]) for p in _halves(p_ref)]
    for i, (x2, f, ple) in enumerate(zip(x2s, ffn, ples)):
        h = ALPHA * x2 + 0.5 * f
        gate = _sigmoid(_dot(h.astype(BF16), wg_ref[...]) + bg_ref[...])
        out_ref[i * SUB_TILE:(i + 1) * SUB_TILE, :] = _layer_norm(h + gate * ple, g3_ref[...], b3_ref[...], LN_EPS)


def _mix_kernel(t_valid, pos0,
                u_ref, zr_ref, hist_ref, shift_ref, s0_ref,
                mu_ref, w0_ref, w2_ref, a0_ref, a2_ref, g2_ref, kk_ref, ka_ref, rk_ref,
                lng_ref, lnb_ref, pw_ref, ps_ref,
                out_ref, s_ref,
                uext_ref, carry_ref, prev_ref, ops_ref, aux_ref, pool_ref, pl_ref):
    c = pl.program_id(1)
    refs = (u_ref, zr_ref, mu_ref, w0_ref, w2_ref, a0_ref, a2_ref, g2_ref, kk_ref, ka_ref, rk_ref,
            lng_ref, lnb_ref, pw_ref, ps_ref, out_ref, s_ref,
            uext_ref, carry_ref, prev_ref, ops_ref, aux_ref, pool_ref, pl_ref)

    @pl.when(c == 0)
    def _():
        carry_ref[1] = hist_ref[...]
        prev_ref[1, :, F32_SUBLANES - 1:F32_SUBLANES, :] = shift_ref[...]
        s_ref[...] = s0_ref[...]
        ops_ref[1] = jnp.zeros(ops_ref.shape[1:], BF16)
        aux_ref[1] = jnp.zeros(aux_ref.shape[1:], F32)
        pool_ref[1] = jnp.zeros(pool_ref.shape[1:], F32)
        pl_ref[1] = jnp.ones(pl_ref.shape[1:], F32)

    @pl.when((c & 1) == 0)
    def _():
        _mix_step(t_valid, pos0, 1, 0, c, *refs)

    @pl.when((c & 1) == 1)
    def _():
        _mix_step(t_valid, pos0, 0, 1, c, *refs)


def _mix_step(t_valid, pos0, rd, wr, c,
              u_ref, zr_ref, mu_ref, w0_ref, w2_ref, a0_ref, a2_ref, g2_ref, kk_ref, ka_ref, rk_ref,
              lng_ref, lnb_ref, pw_ref, ps_ref, out_ref, s_ref,
              uext_ref, carry_ref, prev_ref, ops_ref, aux_ref, pool_ref, pl_ref):
    L = CHUNK
    RL = MIX_ROWS * L
    row_l = lax.broadcasted_iota(jnp.int32, (L, 1), 0)
    row = lax.broadcasted_iota(jnp.int32, (RL, 1), 0)
    rin = row & (L - 1)
    lane = lax.broadcasted_iota(jnp.int32, (1, LANES), 1)
    left = lane < HEAD_DIM
    col = lane & (HEAD_DIM - 1)
    strict = col < row_l
    incl = col <= row_l
    eye = (col == row_l).astype(F32)
    ones_bd = ((lax.broadcasted_iota(jnp.int32, (LANES, LANES), 0) < HEAD_DIM)
               == (lax.broadcasted_iota(jnp.int32, (LANES, LANES), 1) < HEAD_DIM)).astype(BF16)
    sls = [slice(p * LANES, (p + 1) * LANES) for p in range(N_PAIRS)]
    rws = [slice(r * L, (r + 1) * L) for r in range(MIX_ROWS)]
    chains = [(rw, sl) for rw in rws for sl in sls]
    ids = range(len(chains))

    def by_row(vals):
        out = vals[-1]
        for r in range(MIX_ROWS - 2, -1, -1):
            out = jnp.where(row < (r + 1) * L, vals[r], out)
        return out

    def head_sum(x):
        n = x.shape[0]
        tiles = jnp.concatenate([x[:, sl] for sl in sls], axis=0)
        sums = _dot(tiles.astype(BF16), ones_bd)
        return jnp.concatenate([sums[p * n:(p + 1) * n] for p in range(N_PAIRS)], axis=1)

    def bd(x):
        z = jnp.zeros_like(x)
        return jnp.concatenate([jnp.where(left, x, z), jnp.where(left, z, x)], axis=0)

    def pmm(a_, x_):
        return _dot(a_.astype(BF16), bd(x_.astype(BF16)))

    at, rt, bt, kt = ops_ref[rd, 0], ops_ref[rd, 1], ops_ref[rd, 2], ops_ref[rd, 3]
    s_old = [s_ref[r][:, sl] for r in range(MIX_ROWS) for sl in sls]
    big = [lax.dot_general(jnp.concatenate([at[rw, sl], rt[rw, sl]], axis=0),
                           jnp.concatenate([bd(bt[rw, sl]), bd(kt[rw, sl]), bd(s_old[i].astype(BF16))], axis=0),
                           _NT, preferred_element_type=F32) for i, (rw, sl) in enumerate(chains)]

    pooled = []
    for r in range(MIX_ROWS):
        u = u_ref[r].astype(F32)
        uext_ref[r, 0:POOL_PAD, :] = carry_ref[rd, r]
        uext_ref[r, POOL_PAD:POOL_PAD + L, :] = u
        carry_ref[wr, r] = u[L - POOL_PAD:L, :]
        run = u
        sums = {}
        for j in range(1, max(POOL_WINDOWS)):
            run = run + uext_ref[r, POOL_PAD - j:POOL_PAD - j + L, :]
            if j + 1 in POOL_WINDOWS:
                sums[j + 1] = run
        pos1 = pos0 + c * L + row_l + 1
        lane_u = lax.broadcasted_iota(jnp.int32, (1, D_POOL), 1)
        means = sums[16] / jnp.minimum(pos1, 16).astype(F32)
        for gi in (2, 1, 0):
            w = POOL_WINDOWS[gi]
            means = jnp.where(lane_u < (gi + 1) * HEAD_DIM, sums[w] / jnp.minimum(pos1, w).astype(F32), means)
        pooled.append(means - u)
    pool_out = _dot(jnp.concatenate(pooled, axis=0).astype(BF16), pw_ref[...]) * ps_ref[...]

    zr = zr_ref[...].reshape(RL, D_SHIFT).astype(F32)
    last = slice(F32_SUBLANES - 1, F32_SUBLANES)
    zprev = jnp.where(rin == 0, by_row([prev_ref[rd, r, last, :] for r in range(MIX_ROWS)]), pltpu.roll(zr, 1, 0))
    for r in range(MIX_ROWS):
        prev_ref[wr, r] = zr[(r + 1) * L - F32_SUBLANES:(r + 1) * L, :]
    zs = zr + (zprev - zr) * mu_ref[...]
    r_ = zs[:, 0:D_RWKV]
    k = zs[:, D_RWKV:2 * D_RWKV]
    v = zs[:, 2 * D_RWKV:3 * D_RWKV]
    lwa = zs[:, 3 * D_RWKV:3 * D_RWKV + LANES]
    lg = zs[:, 3 * D_RWKV + LANES:]
    tl = jnp.where(left, jnp.tanh(lwa), lwa).astype(BF16)
    logw = -math.exp(-0.5) * _sigmoid(w0_ref[...] + _dot(tl, w2_ref[...]))
    a = _sigmoid(a0_ref[...] + _dot(tl, a2_ref[...]))
    g = _dot(_sigmoid(lg).astype(BF16), g2_ref[...])

    vb = ops_ref[rd, 6]
    n_ = [jnp.where(strict, x[0:L, 0:2 * L], 0.0) for x in big]
    wb = [jnp.where(incl, x[L:2 * L, 0:2 * L], 0.0) for x in big]
    m_ = [jnp.where(strict, x[0:L, 2 * L:4 * L], 0.0) for x in big]
    wk = [jnp.where(incl, x[L:2 * L, 2 * L:4 * L], 0.0) for x in big]
    v_p = [vb[rw, sl] for rw, sl in chains]
    mwv = [_dot(jnp.concatenate([m_[i], wk[i]], axis=0).astype(BF16), bd(v_p[i])) for i in ids]
    tm = [eye + x for x in n_]
    pw = [pmm(x, x) for x in n_]
    n_stages = int(math.log2(L))

    def solve_stage(stage, tm, pw):
        last = stage == n_stages
        prod = [_dot((tm[i] if last else jnp.concatenate([tm[i], pw[i]], axis=0)).astype(BF16),
                     bd(pw[i].astype(BF16))) for i in ids]
        tm = [tm[i] + prod[i][0:L] for i in ids]
        return tm, (None if last else [x[L:2 * L] for x in prod])

    kkr = k * kk_ref[...]
    kk = kkr * jnp.minimum(lax.rsqrt(head_sum(kkr * kkr)), 1e12)
    k2 = k * (1.0 + (a - 1.0) * ka_ref[...])
    b = kk * a
    if t_valid < L:
        valid = rin < t_valid
        logw = jnp.where(valid, logw, 0.0)
        b = jnp.where(valid, b, 0.0)
        k2 = jnp.where(valid, k2, 0.0)
    ri = lax.broadcasted_iota(jnp.int32, (RL, RL), 0)
    ci = lax.broadcasted_iota(jnp.int32, (RL, RL), 1)
    same_row = (ri & -L) == (ci & -L)
    sel = jnp.concatenate([(ri >= ci) & same_row, same_row], axis=0).astype(BF16)
    hi = logw.astype(BF16)
    lo = (logw - hi.astype(F32)).astype(BF16)
    sums = _dot(sel, hi) + _dot(sel, lo)
    cum = sums[0:RL]
    cum_lb = sums[RL:]

    tm, pw = solve_stage(2, tm, pw)
    tm, pw = solve_stage(3, tm, pw)

    e_inv = jnp.exp(-cum)
    e_tail = jnp.exp(cum_lb - cum)
    new_ops = [(-kk * jnp.exp(cum - logw)).astype(BF16), (r_ * jnp.exp(cum)).astype(BF16),
               (b * e_inv).astype(BF16), (k2 * e_inv).astype(BF16),
               (b * e_tail).astype(BF16), (k2 * e_tail).astype(BF16), v.astype(BF16)]
    bonus_v = head_sum(r_ * k2 * rk_ref[...]) * v

    for stage in range(4, n_stages + 1):
        tm, pw = solve_stage(stage, tm, pw)
    u_b = [pmm(tm[i], big[i][0:L, 4 * L:] + mwv[i][0:L]).astype(BF16) for i in ids]
    ys = [big[i][L:2 * L, 4 * L:] + mwv[i][L:2 * L] + pmm(wb[i], u_b[i]) for i in ids]
    bh, kh = ops_ref[rd, 4], ops_ref[rd, 5]
    full = [lax.dot_general(jnp.concatenate([u_b[i], v_p[i]], axis=0),
                            jnp.concatenate([bh[rw, sl], kh[rw, sl]], axis=0),
                            _TN, preferred_element_type=F32) for i, (rw, sl) in enumerate(chains)]
    for r in range(MIX_ROWS):
        p_l = pl_ref[rd, r, 0:1, :]
        s_new = jnp.concatenate(
            [s_old[r * N_PAIRS + p] * p_l[:, sl]
             + jnp.where(left, full[r * N_PAIRS + p][0:HEAD_DIM], full[r * N_PAIRS + p][HEAD_DIM:])
             for p, sl in enumerate(sls)], axis=1)
        s_ref[r] = s_new
    y = jnp.concatenate([jnp.concatenate(ys[r * N_PAIRS:(r + 1) * N_PAIRS], axis=1) for r in range(MIX_ROWS)], axis=0)

    mean = head_sum(y) * (1.0 / HEAD_DIM)
    yc = y - mean
    var = head_sum(yc * yc) * (1.0 / HEAD_DIM)
    yn = yc * lax.rsqrt(var + LNX_EPS) * lng_ref[...] + lnb_ref[...]
    rwkv_out = (yn + aux_ref[rd, 0]) * aux_ref[rd, 1]
    out_ref[...] = jnp.concatenate([pool_ref[rd], rwkv_out], axis=1).astype(BF16).reshape(MIX_ROWS, L, D_MODEL)

    for i, x in enumerate(new_ops):
        ops_ref[wr, i] = x
    aux_ref[wr, 0] = bonus_v
    aux_ref[wr, 1] = g
    pool_ref[wr] = pool_out
    for r in range(MIX_ROWS):
        pl_ref[wr, r] = jnp.exp(cum_lb[r * L:r * L + F32_SUBLANES, :])


def _const_spec(shape):
    return pl.BlockSpec(shape, lambda *_: (0,) * len(shape), pipeline_mode=pl.Buffered(1))


def _row_spec(width):
    return pl.BlockSpec((TOKEN_TILE, width), lambda i: (i, 0))


def _ffn1_call(x, lw):
    n = x.shape[0]
    consts = (lw["ffn1_w_in"], lw["ffn1_w_out"], lw["ln1_g"], lw["ln1_b"], lw["w_in"])
    return pl.pallas_call(
        _ffn1_kernel,
        grid=(n // TOKEN_TILE,),
        in_specs=[_row_spec(D_MODEL)] + [_const_spec(c.shape) for c in consts],
        out_specs=[_row_spec(D_MODEL), _row_spec(D_POOL), _row_spec(D_SHIFT)],
        out_shape=[jax.ShapeDtypeStruct((n, D_MODEL), F32),
                   jax.ShapeDtypeStruct((n, D_POOL), BF16),
                   jax.ShapeDtypeStruct((n, D_SHIFT), BF16)],
        compiler_params=pltpu.CompilerParams(dimension_semantics=("parallel",),
                                             vmem_limit_bytes=V7X_VMEM_BYTES * 7 // 8),
        name="ffn1",
    )(x, *consts)


def _post_call(mix, x1, p, lw):
    n = x1.shape[0]
    consts = (lw["w_out"], lw["ln2_g"], lw["ln2_b"], lw["ffn2_w_in"], lw["ffn2_w_out"],
              lw["ple_gate_w"], lw["ple_gate_b"], lw["ple_w"], lw["ln3_g"], lw["ln3_b"])
    return pl.pallas_call(
        _post_kernel,
        grid=(n // TOKEN_TILE,),
        in_specs=[_row_spec(D_MODEL), _row_spec(D_MODEL), _row_spec(D_PLE)]
        + [_const_spec(c.shape) for c in consts],
        out_specs=_row_spec(D_MODEL),
        out_shape=jax.ShapeDtypeStruct((n, D_MODEL), F32),
        compiler_params=pltpu.CompilerParams(dimension_semantics=("parallel",),
                                             vmem_limit_bytes=V7X_VMEM_BYTES * 7 // 8),
        name="post",
    )(mix, x1, p, *consts)


def _mix_call(u, zr, hist, shift, s0, lw, t_valid, pos0):
    bn, tp, _ = u.shape
    nc = tp // CHUNK
    consts = (lw["mu_shift"], lw["w0"], lw["w2"], lw["a0"], lw["a2"], lw["g2"], lw["k_k"], lw["k_a"],
              lw["r_k"], lw["lnx_g"], lw["lnx_b"], lw["pool_w"], lw["pool_scale"])
    rl = MIX_ROWS * CHUNK
    per_row = lambda shape: pl.BlockSpec((MIX_ROWS,) + shape, lambda bi, ci: (bi, 0, 0))
    chunk_in = lambda width: pl.BlockSpec((MIX_ROWS, CHUNK, width), lambda bi, ci: (bi, jnp.minimum(ci, nc - 1), 0))
    chunk_out = lambda width: pl.BlockSpec((MIX_ROWS, CHUNK, width), lambda bi, ci: (bi, jnp.maximum(ci - 1, 0), 0))
    return pl.pallas_call(
        functools.partial(_mix_kernel, t_valid, pos0),
        grid=(bn // MIX_ROWS, nc + 1),
        in_specs=[chunk_in(D_POOL), chunk_in(D_SHIFT), per_row((POOL_PAD, D_POOL)),
                  per_row((1, D_SHIFT)), per_row((HEAD_DIM, D_RWKV))]
        + [_const_spec(c.shape) for c in consts],
        out_specs=[chunk_out(D_MODEL), per_row((HEAD_DIM, D_RWKV))],
        out_shape=[jax.ShapeDtypeStruct((bn, tp, D_MODEL), BF16),
                   jax.ShapeDtypeStruct((bn, HEAD_DIM, D_RWKV), F32)],
        scratch_shapes=[pltpu.VMEM((MIX_ROWS, POOL_PAD + CHUNK, D_POOL), F32),
                        pltpu.VMEM((2, MIX_ROWS, POOL_PAD, D_POOL), F32),
                        pltpu.VMEM((2, MIX_ROWS, F32_SUBLANES, D_SHIFT), F32),
                        pltpu.VMEM((2, 7, rl, D_RWKV), BF16),
                        pltpu.VMEM((2, 2, rl, D_RWKV), F32),
                        pltpu.VMEM((2, rl, D_POOL), F32),
                        pltpu.VMEM((2, MIX_ROWS, F32_SUBLANES, D_RWKV), F32)],
        compiler_params=pltpu.CompilerParams(dimension_semantics=("parallel", "arbitrary")),
        name="mix",
    )(u, zr, hist, shift, s0, *consts)


def _prepare_layer(i, W):
    row = lambda name: W[name][i].reshape(1, -1).astype(F32)
    mat = lambda name: W[name][i].astype(BF16)
    zeros = jnp.zeros((HEAD_DIM, D_RWKV), BF16)
    lw = {name: mat(name) for name in ("ffn1_w_in", "ffn1_w_out", "ffn2_w_in", "ffn2_w_out", "w_in", "g2",
                                       "w_out", "ple_gate_w", "ple_w")}
    lw.update({name: row(name) for name in ("ln1_g", "ln1_b", "ln2_g", "ln2_b", "ln3_g", "ln3_b", "mu_shift",
                                            "pool_scale", "w0", "a0", "k_k", "k_a", "r_k", "lnx_g", "lnx_b",
                                            "ple_gate_b")})
    lw["w2"] = jnp.concatenate([mat("w2"), zeros], axis=0)
    lw["a2"] = jnp.concatenate([zeros, mat("a2")], axis=0)
    lw["pool_w"] = jax.scipy.linalg.block_diag(*[W["pool_w"][i, gi] for gi in range(len(POOL_WINDOWS))]).astype(BF16)
    return lw


def _run_trunk(x, p, pos0, pool_hist, shift_prev, wkv_prev, layers):
    bn, t, _ = x.shape
    n = bn * t
    tp = -(-t // CHUNK) * CHUNK
    assert tp == t or t < CHUNK, "time padding is only supported for a single partial chunk"
    xf = x.reshape(n, D_MODEL)
    pools, shifts, wkvs = [], [], []
    for i, lw in enumerate(layers):
        x1, u, zr = _ffn1_call(xf, lw)
        u = u.reshape(bn, t, D_POOL)
        zr = zr.reshape(bn, t, D_SHIFT)
        hist = jnp.zeros((bn, POOL_HIST, D_POOL), F32) if pool_hist is None else pool_hist[i]
        pools.append(jnp.concatenate([hist, u], axis=1)[:, -POOL_HIST:])
        shifts.append(zr[:, -1].astype(F32))
        if tp != t:
            u = jnp.pad(u, ((0, 0), (0, tp - t), (0, 0)))
            zr = jnp.pad(zr, ((0, 0), (0, tp - t), (0, 0)))
        hist = jnp.pad(hist, ((0, 0), (POOL_PAD - POOL_HIST, 0), (0, 0)))
        s0 = jnp.transpose(wkv_prev[i], (0, 2, 1, 3)).reshape(bn, HEAD_DIM, D_RWKV)
        mix, s_new = _mix_call(u, zr, hist, shift_prev[i][:, None, :], s0, lw, min(t, CHUNK), pos0)
        wkvs.append(jnp.transpose(s_new.reshape(bn, HEAD_DIM, N_HEADS, HEAD_DIM), (0, 2, 1, 3)))
        xf = _post_call(mix[:, :t].reshape(n, D_MODEL), x1, p[i].reshape(n, D_PLE), lw)
    return xf.reshape(bn, t, D_MODEL), jnp.stack(pools), jnp.stack(shifts), jnp.stack(wkvs)


def kernel(x_prompt, x_sample, p_prompt, p_sample, state_pool, state_shift, state_wkv, ln1_g, ln1_b, ln2_g, ln2_b, ln3_g, ln3_b, ffn1_w_in, ffn1_w_out, ffn2_w_in, ffn2_w_out, w_in, mu_shift, pool_w, pool_scale, w0, w2, a0, a2, g2, k_k, k_a, r_k, lnx_g, lnx_b, w_out, ple_gate_w, ple_gate_b, ple_w):
    W = dict(ln1_g=ln1_g, ln1_b=ln1_b, ln2_g=ln2_g, ln2_b=ln2_b, ln3_g=ln3_g, ln3_b=ln3_b,
             ffn1_w_in=ffn1_w_in, ffn1_w_out=ffn1_w_out, ffn2_w_in=ffn2_w_in, ffn2_w_out=ffn2_w_out,
             w_in=w_in, mu_shift=mu_shift, pool_w=pool_w, pool_scale=pool_scale, w0=w0, w2=w2,
             a0=a0, a2=a2, g2=g2, k_k=k_k, k_a=k_a, r_k=r_k, lnx_g=lnx_g, lnx_b=lnx_b,
             w_out=w_out, ple_gate_w=ple_gate_w, ple_gate_b=ple_gate_b, ple_w=ple_w)
    layers = [_prepare_layer(i, W) for i in range(DEPTH)]
    bn = x_prompt.shape[0]
    y_prompt, pool_prompt, shift_prompt, wkv_prompt = _run_trunk(
        x_prompt, p_prompt, 0, None,
        jnp.zeros((DEPTH, bn, D_SHIFT), F32),
        jnp.zeros((DEPTH, bn, N_HEADS, HEAD_DIM, HEAD_DIM), F32), layers)
    y_sample, pool_sample, shift_sample, wkv_sample = _run_trunk(
        x_sample, p_sample, PAST_LEN, state_pool, state_shift, state_wkv, layers)
    return (y_prompt, y_sample, pool_prompt, shift_prompt, wkv_prompt, pool_sample, shift_sample, wkv_sample)
```

```python
import functools
import math

import jax
import jax.numpy as jnp
from jax import lax
from jax.experimental import pallas as pl
from jax.experimental.pallas import tpu as pltpu

D_MODEL = 1024
D_POOL = 256
D_RWKV = 768
HEAD_DIM = 64
N_HEADS = 12
LANES = 128
N_PAIRS = D_RWKV // LANES
D_SHIFT = 2560
D_FF = 2816
D_PLE = 256
POOL_WINDOWS = (2, 4, 8, 16)
POOL_HIST = 15
POOL_PAD = 16
DEPTH = 2
ALPHA = (2.0 * DEPTH) ** 0.25
LN_EPS = 1e-5
LNX_EPS = 64e-5
PAST_LEN = 1024
CHUNK = 64
N_STAGES = int(math.log2(CHUNK))
TOKEN_TILE = 512
SUB_TILE = TOKEN_TILE // 2
F32_SUBLANES = 8
MIX_ROWS = 2
V7X_VMEM_BYTES = 64 * 1024 * 1024

F32 = jnp.float32
BF16 = jnp.bfloat16
_NT = (((1,), (1,)), ((), ()))
_TN = (((0,), (0,)), ((), ()))


def _dot(a, b):
    return jnp.dot(a, b, preferred_element_type=F32)


def _sigmoid(x):
    return 0.5 * jnp.tanh(0.5 * x) + 0.5


def _layer_norm(h, g, b, eps):
    mu = jnp.mean(h, axis=-1, keepdims=True)
    d = h - mu
    var = jnp.mean(d * d, axis=-1, keepdims=True)
    return d * lax.rsqrt(var + eps) * g + b


def _swiglu_halves(xbs, w_in_ref, w_out_ref):
    hs = [(_dot(xb, w_in_ref[:, :D_FF]), _dot(xb, w_in_ref[:, D_FF:])) for xb in xbs]
    return [_dot((gate * _sigmoid(gate) * up).astype(BF16), w_out_ref[...]) for gate, up in hs]


def _halves(ref):
    return [ref[0:SUB_TILE, :], ref[SUB_TILE:TOKEN_TILE, :]]


def _ffn1_kernel(x_ref, w_in_ref, w_out_ref, g_ref, b_ref, wz_ref, x1_ref, u_ref, zr_ref):
    xs = _halves(x_ref)
    ffn = _swiglu_halves([x.astype(BF16) for x in xs], w_in_ref, w_out_ref)
    for i, (x, f) in enumerate(zip(xs, ffn)):
        rows = slice(i * SUB_TILE, (i + 1) * SUB_TILE)
        x1 = _layer_norm(ALPHA * x + 0.5 * f, g_ref[...], b_ref[...], LN_EPS)
        x1_ref[rows, :] = x1
        z = _dot(x1.astype(BF16), wz_ref[...]).astype(BF16)
        u_ref[rows, :] = z[:, :D_POOL]
        zr_ref[rows, :] = z[:, D_POOL:]


def _post_kernel(mix_ref, x1_ref, p_ref, wo_ref, g2_ref, b2_ref, w_in_ref, w_out_ref,
                 wg_ref, bg_ref, wp_ref, g3_ref, b3_ref, out_ref):
    mixes = [_dot(m, wo_ref[...]) for m in _halves(mix_ref)]
    x2s = [_layer_norm(ALPHA * x1 + mix, g2_ref[...], b2_ref[...], LN_EPS) for x1, mix in zip(_halves(x1_ref), mixes)]
    ffn = _swiglu_halves([x2.astype(BF16) for x2 in x2s], w_in_ref, w_out_ref)
    ples = [_dot(p.astype(BF16), wp_ref[...]) for p in _halves(p_ref)]
    for i, (x2, f, ple) in enumerate(zip(x2s, ffn, ples)):
        h = ALPHA * x2 + 0.5 * f
        gate = _sigmoid(_dot(h.astype(BF16), wg_ref[...]) + bg_ref[...])
        out_ref[i * SUB_TILE:(i + 1) * SUB_TILE, :] = _layer_norm(h + gate * ple, g3_ref[...], b3_ref[...], LN_EPS)


def _mix_kernel(t_valid, pos0,
                u_ref, zr_ref, hist_ref, shift_ref, s0_ref,
                mu_ref, w0_ref, w2_ref, a0_ref, a2_ref, g2_ref, kk_ref, ka_ref, rk_ref,
                lng_ref, lnb_ref, pw_ref, ps_ref,
                out_ref, s_ref,
                uext_ref, carry_ref, prev_ref, opsb_ref, opsf_ref, pool_ref, pl_ref):
    c = pl.program_id(1)
    refs = (u_ref, zr_ref, mu_ref, w0_ref, w2_ref, a0_ref, a2_ref, g2_ref, kk_ref, ka_ref, rk_ref,
            lng_ref, lnb_ref, pw_ref, ps_ref, out_ref, s_ref,
            uext_ref, carry_ref, prev_ref, opsb_ref, opsf_ref, pool_ref, pl_ref)

    @pl.when(c == 0)
    def _():
        carry_ref[1] = hist_ref[...]
        prev_ref[1, :, F32_SUBLANES - 1:F32_SUBLANES, :] = shift_ref[...]
        s_ref[...] = s0_ref[...]
        opsb_ref[1] = jnp.zeros(opsb_ref.shape[1:], BF16)
        opsf_ref[1] = jnp.zeros(opsf_ref.shape[1:], F32)
        pool_ref[1] = jnp.zeros(pool_ref.shape[1:], F32)
        pl_ref[1] = jnp.ones(pl_ref.shape[1:], F32)

    @pl.when((c & 1) == 0)
    def _():
        _mix_step(t_valid, pos0, 1, 0, c, *refs)

    @pl.when((c & 1) == 1)
    def _():
        _mix_step(t_valid, pos0, 0, 1, c, *refs)


def _mix_step(t_valid, pos0, rd, wr, c,
              u_ref, zr_ref, mu_ref, w0_ref, w2_ref, a0_ref, a2_ref, g2_ref, kk_ref, ka_ref, rk_ref,
              lng_ref, lnb_ref, pw_ref, ps_ref, out_ref, s_ref,
              uext_ref, carry_ref, prev_ref, opsb_ref, opsf_ref, pool_ref, pl_ref):
    assert N_STAGES == 6, "the prepare/apply split below places solve stages 2-3 and 4-6 explicitly"
    L = CHUNK
    RL = MIX_ROWS * L
    row_l = lax.broadcasted_iota(jnp.int32, (L, 1), 0)
    row = lax.broadcasted_iota(jnp.int32, (RL, 1), 0)
    rin = row & (L - 1)
    lane = lax.broadcasted_iota(jnp.int32, (1, LANES), 1)
    left = lane < HEAD_DIM
    col = lane & (HEAD_DIM - 1)
    strict = col < row_l
    incl = col <= row_l
    eye = (col == row_l).astype(F32)
    ones_bd = ((lax.broadcasted_iota(jnp.int32, (LANES, LANES), 0) < HEAD_DIM)
               == (lax.broadcasted_iota(jnp.int32, (LANES, LANES), 1) < HEAD_DIM)).astype(BF16)
    sls = [slice(p * LANES, (p + 1) * LANES) for p in range(N_PAIRS)]
    rws = [slice(r * L, (r + 1) * L) for r in range(MIX_ROWS)]
    chains = [(rw, sl) for rw in rws for sl in sls]
    ids = range(len(chains))

    def head_sum(x):
        n = x.shape[0]
        tiles = jnp.concatenate([x[:, sl] for sl in sls], axis=0)
        sums = _dot(tiles.astype(BF16), ones_bd)
        return jnp.concatenate([sums[p * n:(p + 1) * n] for p in range(N_PAIRS)], axis=1)

    def bd(x):
        z = jnp.zeros_like(x)
        return jnp.concatenate([jnp.where(left, x, z), jnp.where(left, z, x)], axis=0)

    def pmm(a_, x_):
        return _dot(a_.astype(BF16), bd(x_.astype(BF16)))

    def assemble(tiles):
        return jnp.concatenate([jnp.concatenate(tiles[r * N_PAIRS:(r + 1) * N_PAIRS], axis=1)
                                for r in range(MIX_ROWS)], axis=0)

    def split(x):
        return [x[rw, sl] for rw, sl in chains]

    def solve_stage(stage, tm, pw):
        last = stage == N_STAGES
        prod = [_dot((tm[i] if last else jnp.concatenate([tm[i], pw[i]], axis=0)).astype(BF16),
                     bd(pw[i].astype(BF16))) for i in ids]
        tm = [tm[i] + prod[i][0:L] for i in ids]
        return tm, (None if last else [x[L:2 * L] for x in prod])

    at_o, rt_o = opsb_ref[rd, 0], opsb_ref[rd, 1]
    s_old = [s_ref[r][:, sl] for r in range(MIX_ROWS) for sl in sls]
    asrs = [lax.dot_general(jnp.concatenate([at_o[rw, sl], rt_o[rw, sl]], axis=0), bd(s_old[i].astype(BF16)),
                            _NT, preferred_element_type=F32) for i, (rw, sl) in enumerate(chains)]

    pooled = []
    for r in range(MIX_ROWS):
        u = u_ref[r].astype(F32)
        uext_ref[r, 0:POOL_PAD, :] = carry_ref[rd, r]
        uext_ref[r, POOL_PAD:POOL_PAD + L, :] = u
        carry_ref[wr, r] = u[L - POOL_PAD:L, :]
        run = u
        sums = {}
        for j in range(1, max(POOL_WINDOWS)):
            run = run + uext_ref[r, POOL_PAD - j:POOL_PAD - j + L, :]
            if j + 1 in POOL_WINDOWS:
                sums[j + 1] = run
        pos1 = pos0 + c * L + row_l + 1
        lane_u = lax.broadcasted_iota(jnp.int32, (1, D_POOL), 1)
        means = sums[16] / jnp.minimum(pos1, 16).astype(F32)
        for gi in (2, 1, 0):
            w = POOL_WINDOWS[gi]
            means = jnp.where(lane_u < (gi + 1) * HEAD_DIM, sums[w] / jnp.minimum(pos1, w).astype(F32), means)
        pooled.append(means - u)
    pool_out = _dot(jnp.concatenate(pooled, axis=0).astype(BF16), pw_ref[...]) * ps_ref[...]

    zr = zr_ref[...].reshape(RL, D_SHIFT).astype(F32)
    last = slice(F32_SUBLANES - 1, F32_SUBLANES)
    rolled = pltpu.roll(zr, 1, 0)
    first = lax.broadcasted_iota(jnp.int32, (F32_SUBLANES, 1), 0) == 0
    parts = []
    for r in range(MIX_ROWS):
        parts += [jnp.where(first, prev_ref[rd, r, last, :], rolled[r * L:r * L + F32_SUBLANES]),
                  rolled[r * L + F32_SUBLANES:(r + 1) * L]]
    zprev = jnp.concatenate(parts, axis=0)
    for r in range(MIX_ROWS):
        prev_ref[wr, r] = zr[(r + 1) * L - F32_SUBLANES:(r + 1) * L, :]
    zs = zr + (zprev - zr) * mu_ref[...]
    r_ = zs[:, 0:D_RWKV]
    k = zs[:, D_RWKV:2 * D_RWKV]
    v = zs[:, 2 * D_RWKV:3 * D_RWKV]
    lwa = zs[:, 3 * D_RWKV:3 * D_RWKV + LANES]
    lg = zs[:, 3 * D_RWKV + LANES:]
    tl = jnp.where(left, jnp.tanh(lwa), lwa).astype(BF16)
    logw = -math.exp(-0.5) * _sigmoid(w0_ref[...] + _dot(tl, w2_ref[...]))
    a = _sigmoid(a0_ref[...] + _dot(tl, a2_ref[...]))
    g = _dot(_sigmoid(lg).astype(BF16), g2_ref[...])

    tm_o, pw_o = solve_stage(N_STAGES - 2, split(opsf_ref[rd, 0]), split(opsf_ref[rd, 1]))

    kkr = k * kk_ref[...]
    kk = kkr * jnp.minimum(jnp.exp(-0.5 * jnp.log(head_sum(kkr * kkr))), 1e12)
    k2 = k * (1.0 + (a - 1.0) * ka_ref[...])
    b = kk * a
    if t_valid < L:
        valid = rin < t_valid
        logw = jnp.where(valid, logw, 0.0)
        b = jnp.where(valid, b, 0.0)
        k2 = jnp.where(valid, k2, 0.0)
    ri = lax.broadcasted_iota(jnp.int32, (RL, RL), 0)
    ci = lax.broadcasted_iota(jnp.int32, (RL, RL), 1)
    same_row = (ri & -L) == (ci & -L)
    sel = jnp.concatenate([(ri >= ci) & same_row, same_row], axis=0).astype(BF16)
    hi = logw.astype(BF16)
    lo = (logw - hi.astype(F32)).astype(BF16)
    sums = _dot(sel, hi) + _dot(sel, lo)
    cum = sums[0:RL]
    cum_lb = sums[RL:]

    tm_o, pw_o = solve_stage(N_STAGES - 1, tm_o, pw_o)

    e_cum = jnp.exp(cum)
    e_inv = jnp.exp(-cum)
    e_prev = jnp.where(rin == 0, 1.0, pltpu.roll(e_cum, 1, 0))
    p_new = [jnp.exp(cum_lb[r * L:r * L + F32_SUBLANES, :]) for r in range(MIX_ROWS)]
    e_tail = jnp.concatenate([x for x in p_new for _ in range(L // F32_SUBLANES)], axis=0) * e_inv
    at = (-kk * e_prev).astype(BF16)
    rt = (r_ * e_cum).astype(BF16)
    bt = (b * e_inv).astype(BF16)
    kt = (k2 * e_inv).astype(BF16)
    vb = v.astype(BF16)
    sc = [lax.dot_general(jnp.concatenate([at[rw, sl], rt[rw, sl]], axis=0),
                          jnp.concatenate([bd(bt[rw, sl]), bd(kt[rw, sl])], axis=0),
                          _NT, preferred_element_type=F32) for rw, sl in chains]
    n_ = [jnp.where(strict, x[0:L, 0:2 * L], 0.0) for x in sc]
    wb = [jnp.where(incl, x[L:2 * L, 0:2 * L], 0.0) for x in sc]
    m_ = [jnp.where(strict, x[0:L, 2 * L:4 * L], 0.0) for x in sc]
    wk = [jnp.where(incl, x[L:2 * L, 2 * L:4 * L], 0.0) for x in sc]

    tm_o, _ = solve_stage(N_STAGES, tm_o, pw_o)

    v_n = split(vb)
    mwv = [_dot(jnp.concatenate([m_[i], wk[i]], axis=0).astype(BF16), bd(v_n[i])) for i in ids]
    tm = [eye + x for x in n_]
    pw = [pmm(x, x) for x in n_]

    mv_o, wkv_o = split(opsf_ref[rd, 2]), split(opsf_ref[rd, 3])
    u_b = [pmm(tm_o[i], asrs[i][0:L] + mv_o[i]).astype(BF16) for i in ids]

    bonus_v = head_sum(r_ * k2 * rk_ref[...]) * v
    tm, pw = solve_stage(2, tm, pw)

    wb_o = split(opsb_ref[rd, 5])
    ys = [asrs[i][L:2 * L] + wkv_o[i] + pmm(wb_o[i], u_b[i]) for i in ids]
    bh_o, kh_o, v_o = opsb_ref[rd, 2], opsb_ref[rd, 3], opsb_ref[rd, 4]
    full = [lax.dot_general(jnp.concatenate([u_b[i], v_o[rw, sl]], axis=0),
                            jnp.concatenate([bh_o[rw, sl], kh_o[rw, sl]], axis=0),
                            _TN, preferred_element_type=F32) for i, (rw, sl) in enumerate(chains)]
    for r in range(MIX_ROWS):
        p_l = pl_ref[rd, r, 0:1, :]
        s_ref[r] = jnp.concatenate(
            [s_old[r * N_PAIRS + p] * p_l[:, sl]
             + jnp.where(left, full[r * N_PAIRS + p][0:HEAD_DIM], full[r * N_PAIRS + p][HEAD_DIM:])
             for p, sl in enumerate(sls)], axis=1)
    y = assemble(ys)

    tm, pw = solve_stage(3, tm, pw)

    mean = head_sum(y) * (1.0 / HEAD_DIM)
    yc = y - mean
    var = head_sum(yc * yc) * (1.0 / HEAD_DIM)
    yn = yc * lax.rsqrt(var + LNX_EPS) * lng_ref[...] + lnb_ref[...]
    rwkv_out = (yn + opsf_ref[rd, 4]) * opsf_ref[rd, 5]
    out_ref[...] = jnp.concatenate([pool_ref[rd], rwkv_out], axis=1).astype(BF16).reshape(MIX_ROWS, L, D_MODEL)

    new_b = [at, rt, (b * e_tail).astype(BF16), (k2 * e_tail).astype(BF16), vb, assemble(wb).astype(BF16)]
    new_f = [assemble(tm), assemble(pw), assemble([x[0:L] for x in mwv]), assemble([x[L:2 * L] for x in mwv]),
             bonus_v, g]
    for i, x in enumerate(new_b):
        opsb_ref[wr, i] = x
    for i, x in enumerate(new_f):
        opsf_ref[wr, i] = x
    pool_ref[wr] = pool_out
    for r in range(MIX_ROWS):
        pl_ref[wr, r] = p_new[r]


def _const_spec(shape):
    return pl.BlockSpec(shape, lambda *_: (0,) * len(shape), pipeline_mode=pl.Buffered(1))


def _row_spec(width):
    return pl.BlockSpec((TOKEN_TILE, width), lambda i: (i, 0))


def _ffn1_call(x, lw):
    n = x.shape[0]
    consts = (lw["ffn1_w_in"], lw["ffn1_w_out"], lw["ln1_g"], lw["ln1_b"], lw["w_in"])
    return pl.pallas_call(
        _ffn1_kernel,
        grid=(n // TOKEN_TILE,),
        in_specs=[_row_spec(D_MODEL)] + [_const_spec(c.shape) for c in consts],
        out_specs=[_row_spec(D_MODEL), _row_spec(D_POOL), _row_spec(D_SHIFT)],
        out_shape=[jax.ShapeDtypeStruct((n, D_MODEL), F32),
                   jax.ShapeDtypeStruct((n, D_POOL), BF16),
                   jax.ShapeDtypeStruct((n, D_SHIFT), BF16)],
        compiler_params=pltpu.CompilerParams(dimension_semantics=("parallel",),
                                             vmem_limit_bytes=V7X_VMEM_BYTES * 7 // 8),
        name="ffn1",
    )(x, *consts)


def _post_call(mix, x1, p, lw):
    n = x1.shape[0]
    consts = (lw["w_out"], lw["ln2_g"], lw["ln2_b"], lw["ffn2_w_in"], lw["ffn2_w_out"],
              lw["ple_gate_w"], lw["ple_gate_b"], lw["ple_w"], lw["ln3_g"], lw["ln3_b"])
    return pl.pallas_call(
        _post_kernel,
        grid=(n // TOKEN_TILE,),
        in_specs=[_row_spec(D_MODEL), _row_spec(D_MODEL), _row_spec(D_PLE)]
        + [_const_spec(c.shape) for c in consts],
        out_specs=_row_spec(D_MODEL),
        out_shape=jax.ShapeDtypeStruct((n, D_MODEL), F32),
        compiler_params=pltpu.CompilerParams(dimension_semantics=("parallel",),
                                             vmem_limit_bytes=V7X_VMEM_BYTES * 7 // 8),
        name="post",
    )(mix, x1, p, *consts)


def _mix_call(u, zr, hist, shift, s0, lw, t_valid, pos0):
    bn, tp, _ = u.shape
    nc = tp // CHUNK
    consts = (lw["mu_shift"], lw["w0"], lw["w2"], lw["a0"], lw["a2"], lw["g2"], lw["k_k"], lw["k_a"],
              lw["r_k"], lw["lnx_g"], lw["lnx_b"], lw["pool_w"], lw["pool_scale"])
    rl = MIX_ROWS * CHUNK
    per_row = lambda shape: pl.BlockSpec((MIX_ROWS,) + shape, lambda bi, ci: (bi, 0, 0))
    chunk_in = lambda width: pl.BlockSpec((MIX_ROWS, CHUNK, width), lambda bi, ci: (bi, jnp.minimum(ci, nc - 1), 0))
    chunk_out = lambda width: pl.BlockSpec((MIX_ROWS, CHUNK, width), lambda bi, ci: (bi, jnp.maximum(ci - 1, 0), 0))
    return pl.pallas_call(
        functools.partial(_mix_kernel, t_valid, pos0),
        grid=(bn // MIX_ROWS, nc + 1),
        in_specs=[chunk_in(D_POOL), chunk_in(D_SHIFT), per_row((POOL_PAD, D_POOL)),
                  per_row((1, D_SHIFT)), per_row((HEAD_DIM, D_RWKV))]
        + [_const_spec(c.shape) for c in consts],
        out_specs=[chunk_out(D_MODEL), per_row((HEAD_DIM, D_RWKV))],
        out_shape=[jax.ShapeDtypeStruct((bn, tp, D_MODEL), BF16),
                   jax.ShapeDtypeStruct((bn, HEAD_DIM, D_RWKV), F32)],
        scratch_shapes=[pltpu.VMEM((MIX_ROWS, POOL_PAD + CHUNK, D_POOL), F32),
                        pltpu.VMEM((2, MIX_ROWS, POOL_PAD, D_POOL), F32),
                        pltpu.VMEM((2, MIX_ROWS, F32_SUBLANES, D_SHIFT), F32),
                        pltpu.VMEM((2, 6, rl, D_RWKV), BF16),
                        pltpu.VMEM((2, 6, rl, D_RWKV), F32),
                        pltpu.VMEM((2, rl, D_POOL), F32),
                        pltpu.VMEM((2, MIX_ROWS, F32_SUBLANES, D_RWKV), F32)],
        compiler_params=pltpu.CompilerParams(dimension_semantics=("parallel", "arbitrary")),
        name="mix",
    )(u, zr, hist, shift, s0, *consts)


def _prepare_layer(i, W):
    row = lambda name: W[name][i].reshape(1, -1).astype(F32)
    mat = lambda name: W[name][i].astype(BF16)
    zeros = jnp.zeros((HEAD_DIM, D_RWKV), BF16)
    lw = {name: mat(name) for name in ("ffn1_w_in", "ffn1_w_out", "ffn2_w_in", "ffn2_w_out", "w_in", "g2",
                                       "w_out", "ple_gate_w", "ple_w")}
    lw.update({name: row(name) for name in ("ln1_g", "ln1_b", "ln2_g", "ln2_b", "ln3_g", "ln3_b", "mu_shift",
                                            "pool_scale", "w0", "a0", "k_k", "k_a", "r_k", "lnx_g", "lnx_b",
                                            "ple_gate_b")})
    lw["w2"] = jnp.concatenate([mat("w2"), zeros], axis=0)
    lw["a2"] = jnp.concatenate([zeros, mat("a2")], axis=0)
    lw["pool_w"] = jax.scipy.linalg.block_diag(*[W["pool_w"][i, gi] for gi in range(len(POOL_WINDOWS))]).astype(BF16)
    return lw


def _run_trunk(x, p, pos0, pool_hist, shift_prev, wkv_prev, layers):
    bn, t, _ = x.shape
    n = bn * t
    tp = -(-t // CHUNK) * CHUNK
    assert tp == t or t < CHUNK, "time padding is only supported for a single partial chunk"
    xf = x.reshape(n, D_MODEL)
    pools, shifts, wkvs = [], [], []
    for i, lw in enumerate(layers):
        x1, u, zr = _ffn1_call(xf, lw)
        u = u.reshape(bn, t, D_POOL)
        zr = zr.reshape(bn, t, D_SHIFT)
        hist = jnp.zeros((bn, POOL_HIST, D_POOL), F32) if pool_hist is None else pool_hist[i]
        pools.append(jnp.concatenate([hist, u], axis=1)[:, -POOL_HIST:])
        shifts.append(zr[:, -1].astype(F32))
        if tp != t:
            u = jnp.pad(u, ((0, 0), (0, tp - t), (0, 0)))
            zr = jnp.pad(zr, ((0, 0), (0, tp - t), (0, 0)))
        hist = jnp.pad(hist, ((0, 0), (POOL_PAD - POOL_HIST, 0), (0, 0)))
        s0 = jnp.transpose(wkv_prev[i], (0, 2, 1, 3)).reshape(bn, HEAD_DIM, D_RWKV)
        mix, s_new = _mix_call(u, zr, hist, shift_prev[i][:, None, :], s0, lw, min(t, CHUNK), pos0)
        wkvs.append(jnp.transpose(s_new.reshape(bn, HEAD_DIM, N_HEADS, HEAD_DIM), (0, 2, 1, 3)))
        xf = _post_call(mix[:, :t].reshape(n, D_MODEL), x1, p[i].reshape(n, D_PLE), lw)
    return xf.reshape(bn, t, D_MODEL), jnp.stack(pools), jnp.stack(shifts), jnp.stack(wkvs)


def kernel(x_prompt, x_sample, p_prompt, p_sample, state_pool, state_shift, state_wkv, ln1_g, ln1_b, ln2_g, ln2_b, ln3_g, ln3_b, ffn1_w_in, ffn1_w_out, ffn2_w_in, ffn2_w_out, w_in, mu_shift, pool_w, pool_scale, w0, w2, a0, a2, g2, k_k, k_a, r_k, lnx_g, lnx_b, w_out, ple_gate_w, ple_gate_b, ple_w):
    W = dict(ln1_g=ln1_g, ln1_b=ln1_b, ln2_g=ln2_g, ln2_b=ln2_b, ln3_g=ln3_g, ln3_b=ln3_b,
             ffn1_w_in=ffn1_w_in, ffn1_w_out=ffn1_w_out, ffn2_w_in=ffn2_w_in, ffn2_w_out=ffn2_w_out,
             w_in=w_in, mu_shift=mu_shift, pool_w=pool_w, pool_scale=pool_scale, w0=w0, w2=w2,
             a0=a0, a2=a2, g2=g2, k_k=k_k, k_a=k_a, r_k=r_k, lnx_g=lnx_g, lnx_b=lnx_b,
             w_out=w_out, ple_gate_w=ple_gate_w, ple_gate_b=ple_gate_b, ple_w=ple_w)
    layers = [_prepare_layer(i, W) for i in range(DEPTH)]
    bn = x_prompt.shape[0]
    y_prompt, pool_prompt, shift_prompt, wkv_prompt = _run_trunk(
        x_prompt, p_prompt, 0, None,
        jnp.zeros((DEPTH, bn, D_SHIFT), F32),
        jnp.zeros((DEPTH, bn, N_HEADS, HEAD_DIM, HEAD_DIM), F32), layers)
    y_sample, pool_sample, shift_sample, wkv_sample = _run_trunk(
        x_sample, p_sample, PAST_LEN, state_pool, state_shift, state_wkv, layers)
    return (y_prompt, y_sample, pool_prompt, shift_prompt, wkv_prompt, pool_sample, shift_sample, wkv_sample)
```

```python
import functools
import math

import jax
import jax.numpy as jnp
from jax import lax
from jax.experimental import pallas as pl
from jax.experimental.pallas import tpu as pltpu

D_MODEL = 1024
D_POOL = 256
D_RWKV = 768
HEAD_DIM = 64
N_HEADS = 12
LANES = 128
N_PAIRS = D_RWKV // LANES
D_SHIFT = 2560
D_FF = 2816
D_PLE = 256
POOL_WINDOWS = (2, 4, 8, 16)
POOL_HIST = 15
POOL_PAD = 16
DEPTH = 2
ALPHA = (2.0 * DEPTH) ** 0.25
LN_EPS = 1e-5
LNX_EPS = 64e-5
PAST_LEN = 1024
CHUNK = 64
N_STAGES = int(math.log2(CHUNK))
TOKEN_TILE = 512
SUB_TILE = TOKEN_TILE // 2
F32_SUBLANES = 8
MIX_ROWS = 4
V7X_VMEM_BYTES = 64 * 1024 * 1024

F32 = jnp.float32
BF16 = jnp.bfloat16
_NT = (((1,), (1,)), ((), ()))
_TN = (((0,), (0,)), ((), ()))


def _dot(a, b):
    return jnp.dot(a, b, preferred_element_type=F32)


def _sigmoid(x):
    return 0.5 * jnp.tanh(0.5 * x) + 0.5


def _layer_norm(h, g, b, eps):
    mu = jnp.mean(h, axis=-1, keepdims=True)
    d = h - mu
    var = jnp.mean(d * d, axis=-1, keepdims=True)
    return d * lax.rsqrt(var + eps) * g + b


def _swiglu_halves(xbs, w_in_ref, w_out_ref):
    hs = [(_dot(xb, w_in_ref[:, :D_FF]), _dot(xb, w_in_ref[:, D_FF:])) for xb in xbs]
    return [_dot((gate * _sigmoid(gate) * up).astype(BF16), w_out_ref[...]) for gate, up in hs]


def _halves(ref):
    return [ref[0:SUB_TILE, :], ref[SUB_TILE:TOKEN_TILE, :]]


def _ffn1_kernel(x_ref, w_in_ref, w_out_ref, g_ref, b_ref, wz_ref, x1_ref, u_ref, zr_ref):
    xs = _halves(x_ref)
    ffn = _swiglu_halves([x.astype(BF16) for x in xs], w_in_ref, w_out_ref)
    for i, (x, f) in enumerate(zip(xs, ffn)):
        rows = slice(i * SUB_TILE, (i + 1) * SUB_TILE)
        x1 = _layer_norm(ALPHA * x + 0.5 * f, g_ref[...], b_ref[...], LN_EPS)
        x1_ref[rows, :] = x1
        z = _dot(x1.astype(BF16), wz_ref[...]).astype(BF16)
        u_ref[rows, :] = z[:, :D_POOL]
        zr_ref[rows, :] = z[:, D_POOL:]


def _post_kernel(mix_ref, x1_ref, p_ref, wo_ref, g2_ref, b2_ref, w_in_ref, w_out_ref,
                 wg_ref, bg_ref, wp_ref, g3_ref, b3_ref, out_ref):
    mixes = [_dot(m, wo_ref[...]) for m in _halves(mix_ref)]
    x2s = [_layer_norm(ALPHA * x1 + mix, g2_ref[...], b2_ref[...], LN_EPS) for x1, mix in zip(_halves(x1_ref), mixes)]
    ffn = _swiglu_halves([x2.astype(BF16) for x2 in x2s], w_in_ref, w_out_ref)
    ples = [_dot(p.astype(BF16), wp_ref[...]) for p in _halves(p_ref)]
    for i, (x2, f, ple) in enumerate(zip(x2s, ffn, ples)):
        h = ALPHA * x2 + 0.5 * f
        gate = _sigmoid(_dot(h.astype(BF16), wg_ref[...]) + bg_ref[...])
        out_ref[i * SUB_TILE:(i + 1) * SUB_TILE, :] = _layer_norm(h + gate * ple, g3_ref[...], b3_ref[...], LN_EPS)


def _mix_kernel(t_valid, pos0,
                u_ref, zr_ref, hist_ref, shift_ref, s0_ref,
                mu_ref, w0_ref, w2_ref, a0_ref, a2_ref, g2_ref, kk_ref, ka_ref, rk_ref,
                lng_ref, lnb_ref, pw_ref, ps_ref,
                out_ref, s_ref,
                uext_ref, carry_ref, prev_ref, opsb_ref, opsf_ref, pool_ref, pl_ref):
    c = pl.program_id(1)
    refs = (u_ref, zr_ref, mu_ref, w0_ref, w2_ref, a0_ref, a2_ref, g2_ref, kk_ref, ka_ref, rk_ref,
            lng_ref, lnb_ref, pw_ref, ps_ref, out_ref, s_ref,
            uext_ref, carry_ref, prev_ref, opsb_ref, opsf_ref, pool_ref, pl_ref)

    @pl.when(c == 0)
    def _():
        carry_ref[1] = hist_ref[...]
        prev_ref[1, :, F32_SUBLANES - 1:F32_SUBLANES, :] = shift_ref[...]
        s_ref[...] = s0_ref[...]
        opsb_ref[1] = jnp.zeros(opsb_ref.shape[1:], BF16)
        opsf_ref[1] = jnp.zeros(opsf_ref.shape[1:], F32)
        pool_ref[1] = jnp.zeros(pool_ref.shape[1:], F32)
        pl_ref[1] = jnp.ones(pl_ref.shape[1:], F32)

    @pl.when((c & 1) == 0)
    def _():
        _mix_step(t_valid, pos0, 1, 0, c, *refs)

    @pl.when((c & 1) == 1)
    def _():
        _mix_step(t_valid, pos0, 0, 1, c, *refs)


def _mix_step(t_valid, pos0, rd, wr, c,
              u_ref, zr_ref, mu_ref, w0_ref, w2_ref, a0_ref, a2_ref, g2_ref, kk_ref, ka_ref, rk_ref,
              lng_ref, lnb_ref, pw_ref, ps_ref, out_ref, s_ref,
              uext_ref, carry_ref, prev_ref, opsb_ref, opsf_ref, pool_ref, pl_ref):
    assert N_STAGES == 6, "the prepare/apply split below places solve stages 2-3 and 4-6 explicitly"
    L = CHUNK
    RL = MIX_ROWS * L
    row_l = lax.broadcasted_iota(jnp.int32, (L, 1), 0)
    row = lax.broadcasted_iota(jnp.int32, (RL, 1), 0)
    rin = row & (L - 1)
    lane = lax.broadcasted_iota(jnp.int32, (1, LANES), 1)
    left = lane < HEAD_DIM
    col = lane & (HEAD_DIM - 1)
    strict = col < row_l
    incl = col <= row_l
    eye = (col == row_l).astype(F32)
    sls = [slice(p * LANES, (p + 1) * LANES) for p in range(N_PAIRS)]
    rws = [slice(r * L, (r + 1) * L) for r in range(MIX_ROWS)]
    chains = [(rw, sl) for rw in rws for sl in sls]
    ids = range(len(chains))

    ones_bd = ((lax.broadcasted_iota(jnp.int32, (LANES, LANES), 0) < HEAD_DIM)
               == (lax.broadcasted_iota(jnp.int32, (LANES, LANES), 1) < HEAD_DIM)).astype(BF16)

    def head_sum(x):
        n = x.shape[0]
        tiles = jnp.concatenate([x[:, sl] for sl in sls], axis=0)
        sums = _dot(tiles.astype(BF16), ones_bd)
        return jnp.concatenate([sums[p * n:(p + 1) * n] for p in range(N_PAIRS)], axis=1)

    def bd(x):
        z = jnp.zeros_like(x)
        return jnp.concatenate([jnp.where(left, x, z), jnp.where(left, z, x)], axis=0)

    def pmm(a_, x_):
        return _dot(a_.astype(BF16), bd(x_.astype(BF16)))

    def assemble(tiles):
        return jnp.concatenate([jnp.concatenate(tiles[r * N_PAIRS:(r + 1) * N_PAIRS], axis=1)
                                for r in range(MIX_ROWS)], axis=0)

    def split(x):
        return [x[rw, sl] for rw, sl in chains]

    def solve_stage(stage, tm, pw):
        last = stage == N_STAGES
        prod = [_dot((tm[i] if last else jnp.concatenate([tm[i], pw[i]], axis=0)).astype(BF16),
                     bd(pw[i].astype(BF16))) for i in ids]
        tm = [tm[i] + prod[i][0:L] for i in ids]
        return tm, (None if last else [x[L:2 * L] for x in prod])

    at_o, rt_o = opsb_ref[rd, 0], opsb_ref[rd, 1]
    s_old = [s_ref[r][:, sl] for r in range(MIX_ROWS) for sl in sls]
    asrs = [lax.dot_general(jnp.concatenate([at_o[rw, sl], rt_o[rw, sl]], axis=0), bd(s_old[i].astype(BF16)),
                            _NT, preferred_element_type=F32) for i, (rw, sl) in enumerate(chains)]

    pooled = []
    for r in range(MIX_ROWS):
        u = u_ref[r].astype(F32)
        uext_ref[r, 0:POOL_PAD, :] = carry_ref[rd, r]
        uext_ref[r, POOL_PAD:POOL_PAD + L, :] = u
        carry_ref[wr, r] = u[L - POOL_PAD:L, :]
        run = u
        sums = {}
        for j in range(1, max(POOL_WINDOWS)):
            run = run + uext_ref[r, POOL_PAD - j:POOL_PAD - j + L, :]
            if j + 1 in POOL_WINDOWS:
                sums[j + 1] = run
        pos1 = pos0 + c * L + row_l + 1
        lane_u = lax.broadcasted_iota(jnp.int32, (1, D_POOL), 1)
        means = sums[16] / jnp.minimum(pos1, 16).astype(F32)
        for gi in (2, 1, 0):
            w = POOL_WINDOWS[gi]
            means = jnp.where(lane_u < (gi + 1) * HEAD_DIM, sums[w] / jnp.minimum(pos1, w).astype(F32), means)
        pooled.append(means - u)
    pool_out = _dot(jnp.concatenate(pooled, axis=0).astype(BF16), pw_ref[...]) * ps_ref[...]

    zr = zr_ref[...].reshape(RL, D_SHIFT).astype(F32)
    last = slice(F32_SUBLANES - 1, F32_SUBLANES)
    rolled = pltpu.roll(zr, 1, 0)
    first = lax.broadcasted_iota(jnp.int32, (F32_SUBLANES, 1), 0) == 0
    parts = []
    for r in range(MIX_ROWS):
        parts += [jnp.where(first, prev_ref[rd, r, last, :], rolled[r * L:r * L + F32_SUBLANES]),
                  rolled[r * L + F32_SUBLANES:(r + 1) * L]]
    zprev = jnp.concatenate(parts, axis=0)
    for r in range(MIX_ROWS):
        prev_ref[wr, r] = zr[(r + 1) * L - F32_SUBLANES:(r + 1) * L, :]
    zs = zr + (zprev - zr) * mu_ref[...]
    r_ = zs[:, 0:D_RWKV]
    k = zs[:, D_RWKV:2 * D_RWKV]
    v = zs[:, 2 * D_RWKV:3 * D_RWKV]
    lwa = zs[:, 3 * D_RWKV:3 * D_RWKV + LANES]
    lg = zs[:, 3 * D_RWKV + LANES:]
    tl = jnp.where(left, jnp.tanh(lwa), lwa).astype(BF16)
    logw = -math.exp(-0.5) * _sigmoid(w0_ref[...] + _dot(tl, w2_ref[...]))
    a = _sigmoid(a0_ref[...] + _dot(tl, a2_ref[...]))
    g = _dot(_sigmoid(lg).astype(BF16), g2_ref[...])

    tm_o, pw_o = solve_stage(N_STAGES - 2, split(opsf_ref[rd, 0]), split(opsf_ref[rd, 1]))

    kkr = k * kk_ref[...]
    kk = kkr * jnp.minimum(jnp.exp(-0.5 * jnp.log(head_sum(kkr * kkr))), 1e12)
    k2 = k * (1.0 + (a - 1.0) * ka_ref[...])
    b = kk * a
    if t_valid < L:
        valid = rin < t_valid
        logw = jnp.where(valid, logw, 0.0)
        b = jnp.where(valid, b, 0.0)
        k2 = jnp.where(valid, k2, 0.0)
    ri = lax.broadcasted_iota(jnp.int32, (RL, RL), 0)
    ci = lax.broadcasted_iota(jnp.int32, (RL, RL), 1)
    same_row = (ri & -L) == (ci & -L)
    sel = jnp.concatenate([(ri >= ci) & same_row, same_row], axis=0).astype(BF16)
    hi = logw.astype(BF16)
    lo = (logw - hi.astype(F32)).astype(BF16)
    sums = _dot(sel, hi) + _dot(sel, lo)
    cum = sums[0:RL]
    cum_lb = sums[RL:]

    tm_o, pw_o = solve_stage(N_STAGES - 1, tm_o, pw_o)

    e_cum = jnp.exp(cum)
    e_inv = jnp.exp(-cum)
    e_prev = jnp.where(rin == 0, 1.0, pltpu.roll(e_cum, 1, 0))
    p_new = [jnp.exp(cum_lb[r * L:r * L + F32_SUBLANES, :]) for r in range(MIX_ROWS)]
    e_tail = jnp.concatenate([x for x in p_new for _ in range(L // F32_SUBLANES)], axis=0) * e_inv
    at = (-kk * e_prev).astype(BF16)
    rt = (r_ * e_cum).astype(BF16)
    bt = (b * e_inv).astype(BF16)
    kt = (k2 * e_inv).astype(BF16)
    vb = v.astype(BF16)
    sc = [lax.dot_general(jnp.concatenate([at[rw, sl], rt[rw, sl]], axis=0),
                          jnp.concatenate([bd(bt[rw, sl]), bd(kt[rw, sl])], axis=0),
                          _NT, preferred_element_type=F32) for rw, sl in chains]
    n_ = [jnp.where(strict, x[0:L, 0:2 * L], 0.0) for x in sc]
    wb = [jnp.where(incl, x[L:2 * L, 0:2 * L], 0.0) for x in sc]
    m_ = [jnp.where(strict, x[0:L, 2 * L:4 * L], 0.0) for x in sc]
    wk = [jnp.where(incl, x[L:2 * L, 2 * L:4 * L], 0.0) for x in sc]

    tm_o, _ = solve_stage(N_STAGES, tm_o, pw_o)

    v_n = split(vb)
    mwv = [_dot(jnp.concatenate([m_[i], wk[i]], axis=0).astype(BF16), bd(v_n[i])) for i in ids]
    tm = [eye + x for x in n_]
    pw = [pmm(x, x) for x in n_]

    mv_o, wkv_o = split(opsf_ref[rd, 2]), split(opsf_ref[rd, 3])
    u_b = [pmm(tm_o[i], asrs[i][0:L] + mv_o[i]).astype(BF16) for i in ids]

    bonus_v = head_sum(r_ * k2 * rk_ref[...]) * v
    tm, pw = solve_stage(2, tm, pw)

    wb_o = split(opsb_ref[rd, 5])
    ys = [asrs[i][L:2 * L] + wkv_o[i] + pmm(wb_o[i], u_b[i]) for i in ids]
    bh_o, kh_o, v_o = opsb_ref[rd, 2], opsb_ref[rd, 3], opsb_ref[rd, 4]
    full = [lax.dot_general(jnp.concatenate([u_b[i], v_o[rw, sl]], axis=0),
                            jnp.concatenate([bh_o[rw, sl], kh_o[rw, sl]], axis=0),
                            _TN, preferred_element_type=F32) for i, (rw, sl) in enumerate(chains)]
    for r in range(MIX_ROWS):
        p_l = pl_ref[rd, r, 0:1, :]
        s_ref[r] = jnp.concatenate(
            [s_old[r * N_PAIRS + p] * p_l[:, sl]
             + jnp.where(left, full[r * N_PAIRS + p][0:HEAD_DIM], full[r * N_PAIRS + p][HEAD_DIM:])
             for p, sl in enumerate(sls)], axis=1)
    y = assemble(ys)

    tm, pw = solve_stage(3, tm, pw)

    mean = head_sum(y) * (1.0 / HEAD_DIM)
    yc = y - mean
    var = head_sum(yc * yc) * (1.0 / HEAD_DIM)
    yn = yc * lax.rsqrt(var + LNX_EPS) * lng_ref[...] + lnb_ref[...]
    rwkv_out = (yn + opsf_ref[rd, 4]) * opsf_ref[rd, 5]
    out_ref[...] = jnp.concatenate([pool_ref[rd], rwkv_out], axis=1).astype(BF16).reshape(MIX_ROWS, L, D_MODEL)

    new_b = [at, rt, (b * e_tail).astype(BF16), (k2 * e_tail).astype(BF16), vb, assemble(wb).astype(BF16)]
    new_f = [assemble(tm), assemble(pw), assemble([x[0:L] for x in mwv]), assemble([x[L:2 * L] for x in mwv]),
             bonus_v, g]
    for i, x in enumerate(new_b):
        opsb_ref[wr, i] = x
    for i, x in enumerate(new_f):
        opsf_ref[wr, i] = x
    pool_ref[wr] = pool_out
    for r in range(MIX_ROWS):
        pl_ref[wr, r] = p_new[r]


def _const_spec(shape):
    return pl.BlockSpec(shape, lambda *_: (0,) * len(shape), pipeline_mode=pl.Buffered(1))


def _row_spec(width):
    return pl.BlockSpec((TOKEN_TILE, width), lambda i: (i, 0))


def _ffn1_call(x, lw):
    n = x.shape[0]
    consts = (lw["ffn1_w_in"], lw["ffn1_w_out"], lw["ln1_g"], lw["ln1_b"], lw["w_in"])
    return pl.pallas_call(
        _ffn1_kernel,
        grid=(n // TOKEN_TILE,),
        in_specs=[_row_spec(D_MODEL)] + [_const_spec(c.shape) for c in consts],
        out_specs=[_row_spec(D_MODEL), _row_spec(D_POOL), _row_spec(D_SHIFT)],
        out_shape=[jax.ShapeDtypeStruct((n, D_MODEL), F32),
                   jax.ShapeDtypeStruct((n, D_POOL), BF16),
                   jax.ShapeDtypeStruct((n, D_SHIFT), BF16)],
        compiler_params=pltpu.CompilerParams(dimension_semantics=("parallel",),
                                             vmem_limit_bytes=V7X_VMEM_BYTES * 7 // 8),
        name="ffn1",
    )(x, *consts)


def _post_call(mix, x1, p, lw):
    n = x1.shape[0]
    consts = (lw["w_out"], lw["ln2_g"], lw["ln2_b"], lw["ffn2_w_in"], lw["ffn2_w_out"],
              lw["ple_gate_w"], lw["ple_gate_b"], lw["ple_w"], lw["ln3_g"], lw["ln3_b"])
    return pl.pallas_call(
        _post_kernel,
        grid=(n // TOKEN_TILE,),
        in_specs=[_row_spec(D_MODEL), _row_spec(D_MODEL), _row_spec(D_PLE)]
        + [_const_spec(c.shape) for c in consts],
        out_specs=_row_spec(D_MODEL),
        out_shape=jax.ShapeDtypeStruct((n, D_MODEL), F32),
        compiler_params=pltpu.CompilerParams(dimension_semantics=("parallel",),
                                             vmem_limit_bytes=V7X_VMEM_BYTES * 7 // 8),
        name="post",
    )(mix, x1, p, *consts)


def _mix_call(u, zr, hist, shift, s0, lw, t_valid, pos0):
    bn, tp, _ = u.shape
    nc = tp // CHUNK
    consts = (lw["mu_shift"], lw["w0"], lw["w2"], lw["a0"], lw["a2"], lw["g2"], lw["k_k"], lw["k_a"],
              lw["r_k"], lw["lnx_g"], lw["lnx_b"], lw["pool_w"], lw["pool_scale"])
    rl = MIX_ROWS * CHUNK
    per_row = lambda shape: pl.BlockSpec((MIX_ROWS,) + shape, lambda bi, ci: (bi, 0, 0))
    chunk_in = lambda width: pl.BlockSpec((MIX_ROWS, CHUNK, width), lambda bi, ci: (bi, jnp.minimum(ci, nc - 1), 0))
    chunk_out = lambda width: pl.BlockSpec((MIX_ROWS, CHUNK, width), lambda bi, ci: (bi, jnp.maximum(ci - 1, 0), 0))
    return pl.pallas_call(
        functools.partial(_mix_kernel, t_valid, pos0),
        grid=(bn // MIX_ROWS, nc + 1),
        in_specs=[chunk_in(D_POOL), chunk_in(D_SHIFT), per_row((POOL_PAD, D_POOL)),
                  per_row((1, D_SHIFT)), per_row((HEAD_DIM, D_RWKV))]
        + [_const_spec(c.shape) for c in consts],
        out_specs=[chunk_out(D_MODEL), per_row((HEAD_DIM, D_RWKV))],
        out_shape=[jax.ShapeDtypeStruct((bn, tp, D_MODEL), BF16),
                   jax.ShapeDtypeStruct((bn, HEAD_DIM, D_RWKV), F32)],
        scratch_shapes=[pltpu.VMEM((MIX_ROWS, POOL_PAD + CHUNK, D_POOL), F32),
                        pltpu.VMEM((2, MIX_ROWS, POOL_PAD, D_POOL), F32),
                        pltpu.VMEM((2, MIX_ROWS, F32_SUBLANES, D_SHIFT), F32),
                        pltpu.VMEM((2, 6, rl, D_RWKV), BF16),
                        pltpu.VMEM((2, 6, rl, D_RWKV), F32),
                        pltpu.VMEM((2, rl, D_POOL), F32),
                        pltpu.VMEM((2, MIX_ROWS, F32_SUBLANES, D_RWKV), F32)],
        compiler_params=pltpu.CompilerParams(dimension_semantics=("parallel", "arbitrary")),
        name="mix",
    )(u, zr, hist, shift, s0, *consts)


def _prepare_layer(i, W):
    row = lambda name: W[name][i].reshape(1, -1).astype(F32)
    mat = lambda name: W[name][i].astype(BF16)
    zeros = jnp.zeros((HEAD_DIM, D_RWKV), BF16)
    lw = {name: mat(name) for name in ("ffn1_w_in", "ffn1_w_out", "ffn2_w_in", "ffn2_w_out", "w_in", "g2",
                                       "w_out", "ple_gate_w", "ple_w")}
    lw.update({name: row(name) for name in ("ln1_g", "ln1_b", "ln2_g", "ln2_b", "ln3_g", "ln3_b", "mu_shift",
                                            "pool_scale", "w0", "a0", "k_k", "k_a", "r_k", "lnx_g", "lnx_b",
                                            "ple_gate_b")})
    lw["w2"] = jnp.concatenate([mat("w2"), zeros], axis=0)
    lw["a2"] = jnp.concatenate([zeros, mat("a2")], axis=0)
    lw["pool_w"] = jax.scipy.linalg.block_diag(*[W["pool_w"][i, gi] for gi in range(len(POOL_WINDOWS))]).astype(BF16)
    return lw


def _run_trunk(x, p, pos0, pool_hist, shift_prev, wkv_prev, layers):
    bn, t, _ = x.shape
    n = bn * t
    tp = -(-t // CHUNK) * CHUNK
    assert tp == t or t < CHUNK, "time padding is only supported for a single partial chunk"
    xf = x.reshape(n, D_MODEL)
    pools, shifts, wkvs = [], [], []
    for i, lw in enumerate(layers):
        x1, u, zr = _ffn1_call(xf, lw)
        u = u.reshape(bn, t, D_POOL)
        zr = zr.reshape(bn, t, D_SHIFT)
        hist = jnp.zeros((bn, POOL_HIST, D_POOL), F32) if pool_hist is None else pool_hist[i]
        pools.append(jnp.concatenate([hist, u[:, -POOL_HIST:]], axis=1)[:, -POOL_HIST:])
        shifts.append(zr[:, -1].astype(F32))
        if tp != t:
            u = jnp.pad(u, ((0, 0), (0, tp - t), (0, 0)))
            zr = jnp.pad(zr, ((0, 0), (0, tp - t), (0, 0)))
        hist = jnp.pad(hist, ((0, 0), (POOL_PAD - POOL_HIST, 0), (0, 0)))
        s0 = jnp.transpose(wkv_prev[i], (0, 2, 1, 3)).reshape(bn, HEAD_DIM, D_RWKV)
        mix, s_new = _mix_call(u, zr, hist, shift_prev[i][:, None, :], s0, lw, min(t, CHUNK), pos0)
        wkvs.append(jnp.transpose(s_new.reshape(bn, HEAD_DIM, N_HEADS, HEAD_DIM), (0, 2, 1, 3)))
        xf = _post_call(mix[:, :t].reshape(n, D_MODEL), x1, p[i].reshape(n, D_PLE), lw)
    return xf.reshape(bn, t, D_MODEL), jnp.stack(pools), jnp.stack(shifts), jnp.stack(wkvs)


def kernel(x_prompt, x_sample, p_prompt, p_sample, state_pool, state_shift, state_wkv, ln1_g, ln1_b, ln2_g, ln2_b, ln3_g, ln3_b, ffn1_w_in, ffn1_w_out, ffn2_w_in, ffn2_w_out, w_in, mu_shift, pool_w, pool_scale, w0, w2, a0, a2, g2, k_k, k_a, r_k, lnx_g, lnx_b, w_out, ple_gate_w, ple_gate_b, ple_w):
    W = dict(ln1_g=ln1_g, ln1_b=ln1_b, ln2_g=ln2_g, ln2_b=ln2_b, ln3_g=ln3_g, ln3_b=ln3_b,
             ffn1_w_in=ffn1_w_in, ffn1_w_out=ffn1_w_out, ffn2_w_in=ffn2_w_in, ffn2_w_out=ffn2_w_out,
             w_in=w_in, mu_shift=mu_shift, pool_w=pool_w, pool_scale=pool_scale, w0=w0, w2=w2,
             a0=a0, a2=a2, g2=g2, k_k=k_k, k_a=k_a, r_k=r_k, lnx_g=lnx_g, lnx_b=lnx_b,
             w_out=w_out, ple_gate_w=ple_gate_w, ple_gate_b=ple_gate_b, ple_w=ple_w)
    layers = [_prepare_layer(i, W) for i in range(DEPTH)]
    bn = x_prompt.shape[0]
    y_prompt, pool_prompt, shift_prompt, wkv_prompt = _run_trunk(
        x_prompt, p_prompt, 0, None,
        jnp.zeros((DEPTH, bn, D_SHIFT), F32),
        jnp.zeros((DEPTH, bn, N_HEADS, HEAD_DIM, HEAD_DIM), F32), layers)
    y_sample, pool_sample, shift_sample, wkv_sample = _run_trunk(
        x_sample, p_sample, PAST_LEN, state_pool, state_shift, state_wkv, layers)
    return (y_prompt, y_sample, pool_prompt, shift_prompt, wkv_prompt, pool_sample, shift_sample, wkv_sample)
```

```python
import functools
import math

import jax
import jax.numpy as jnp
from jax import lax
from jax.experimental import pallas as pl
from jax.experimental.pallas import tpu as pltpu

D_MODEL = 1024
D_POOL = 256
D_RWKV = 768
HEAD_DIM = 64
N_HEADS = 12
LANES = 128
N_PAIRS = D_RWKV // LANES
D_SHIFT = 2560
D_FF = 2816
D_PLE = 256
POOL_WINDOWS = (2, 4, 8, 16)
POOL_HIST = 15
POOL_PAD = 16
DEPTH = 2
ALPHA = (2.0 * DEPTH) ** 0.25
LN_EPS = 1e-5
LNX_EPS = 64e-5
PAST_LEN = 1024
CHUNK = 64
N_STAGES = int(math.log2(CHUNK))
TOKEN_TILE = 512
SUB_TILE = TOKEN_TILE // 2
F32_SUBLANES = 8
MIX_ROWS = 2
V7X_VMEM_BYTES = 64 * 1024 * 1024
DENSE_VMEM_LIMIT = V7X_VMEM_BYTES * 7 // 8

F32 = jnp.float32
BF16 = jnp.bfloat16
_NT = (((1,), (1,)), ((), ()))
_TN = (((0,), (0,)), ((), ()))


def _dot(a, b):
    return jnp.dot(a, b, preferred_element_type=F32)


def _sigmoid(x):
    return 0.5 * jnp.tanh(0.5 * x) + 0.5


def _layer_norm(h, g, b, eps):
    mu = jnp.mean(h, axis=-1, keepdims=True)
    d = h - mu
    var = jnp.mean(d * d, axis=-1, keepdims=True)
    return d * lax.rsqrt(var + eps) * g + b


def _swiglu_halves(xbs, w_in_ref, w_out_ref):
    hs = [(_dot(xb, w_in_ref[:, :D_FF]), _dot(xb, w_in_ref[:, D_FF:])) for xb in xbs]
    return [_dot((gate * _sigmoid(gate) * up).astype(BF16), w_out_ref[...]) for gate, up in hs]


def _halves(ref):
    return [ref[0:SUB_TILE, :], ref[SUB_TILE:TOKEN_TILE, :]]


def _ffn1_kernel(x_ref, w_in_ref, w_out_ref, g_ref, b_ref, wz_ref, x1_ref, u_ref, zr_ref):
    xs = _halves(x_ref)
    ffn = _swiglu_halves([x.astype(BF16) for x in xs], w_in_ref, w_out_ref)
    for i, (x, f) in enumerate(zip(xs, ffn)):
        rows = slice(i * SUB_TILE, (i + 1) * SUB_TILE)
        x1 = _layer_norm(ALPHA * x + 0.5 * f, g_ref[...], b_ref[...], LN_EPS)
        x1_ref[rows, :] = x1
        z = _dot(x1.astype(BF16), wz_ref[...]).astype(BF16)
        u_ref[rows, :] = z[:, :D_POOL]
        zr_ref[rows, :] = z[:, D_POOL:]


def _post_kernel(mix_ref, x1_ref, p_ref, wo_ref, g2_ref, b2_ref, w_in_ref, w_out_ref,
                 wg_ref, bg_ref, wp_ref, g3_ref, b3_ref, out_ref):
    mixes = [_dot(m, wo_ref[...]) for m in _halves(mix_ref)]
    x2s = [_layer_norm(ALPHA * x1 + mix, g2_ref[...], b2_ref[...], LN_EPS) for x1, mix in zip(_halves(x1_ref), mixes)]
    ffn = _swiglu_halves([x2.astype(BF16) for x2 in x2s], w_in_ref, w_out_ref)
    ples = [_dot(p.astype(BF16), wp_ref[...]) for p in _halves(p_ref)]
    for i, (x2, f, ple) in enumerate(zip(x2s, ffn, ples)):
        h = ALPHA * x2 + 0.5 * f
        gate = _sigmoid(_dot(h.astype(BF16), wg_ref[...]) + bg_ref[...])
        out_ref[i * SUB_TILE:(i + 1) * SUB_TILE, :] = _layer_norm(h + gate * ple, g3_ref[...], b3_ref[...], LN_EPS)


def _mix_kernel(t_valid, pos0,
                u_ref, zr_ref, hist_ref, shift_ref, s0_ref,
                mu_ref, w0_ref, w2_ref, a0_ref, a2_ref, g2_ref, kk_ref, ka_ref, rk_ref,
                lng_ref, lnb_ref, pw_ref, ps_ref,
                out_ref, s_ref,
                uext_ref, carry_ref, prev_ref, opsb_ref, opsf_ref, pool_ref, pl_ref):
    c = pl.program_id(1)
    refs = (u_ref, zr_ref, mu_ref, w0_ref, w2_ref, a0_ref, a2_ref, g2_ref, kk_ref, ka_ref, rk_ref,
            lng_ref, lnb_ref, pw_ref, ps_ref, out_ref, s_ref,
            uext_ref, carry_ref, prev_ref, opsb_ref, opsf_ref, pool_ref, pl_ref)

    @pl.when(c == 0)
    def _():
        carry_ref[1] = hist_ref[...]
        prev_ref[1, :, F32_SUBLANES - 1:F32_SUBLANES, :] = shift_ref[...]
        s_ref[...] = s0_ref[...]
        opsb_ref[1] = jnp.zeros(opsb_ref.shape[1:], BF16)
        opsf_ref[1] = jnp.zeros(opsf_ref.shape[1:], F32)
        pool_ref[1] = jnp.zeros(pool_ref.shape[1:], F32)
        pl_ref[1] = jnp.ones(pl_ref.shape[1:], F32)

    @pl.when((c & 1) == 0)
    def _():
        _mix_step(t_valid, pos0, 1, 0, c, *refs)

    @pl.when((c & 1) == 1)
    def _():
        _mix_step(t_valid, pos0, 0, 1, c, *refs)


def _mix_step(t_valid, pos0, rd, wr, c,
              u_ref, zr_ref, mu_ref, w0_ref, w2_ref, a0_ref, a2_ref, g2_ref, kk_ref, ka_ref, rk_ref,
              lng_ref, lnb_ref, pw_ref, ps_ref, out_ref, s_ref,
              uext_ref, carry_ref, prev_ref, opsb_ref, opsf_ref, pool_ref, pl_ref):
    assert N_STAGES == 6, "the prepare/apply split below places solve stages 2-3 and 4-6 explicitly"
    L = CHUNK
    RL = MIX_ROWS * L
    row_l = lax.broadcasted_iota(jnp.int32, (L, 1), 0)
    row = lax.broadcasted_iota(jnp.int32, (RL, 1), 0)
    rin = row & (L - 1)
    lane = lax.broadcasted_iota(jnp.int32, (1, LANES), 1)
    left = lane < HEAD_DIM
    col = lane & (HEAD_DIM - 1)
    strict = col < row_l
    incl = col <= row_l
    eye = (col == row_l).astype(F32)
    sls = [slice(p * LANES, (p + 1) * LANES) for p in range(N_PAIRS)]
    rws = [slice(r * L, (r + 1) * L) for r in range(MIX_ROWS)]
    chains = [(rw, sl) for rw in rws for sl in sls]
    ids = range(len(chains))

    ones_bd = ((lax.broadcasted_iota(jnp.int32, (LANES, LANES), 0) < HEAD_DIM)
               == (lax.broadcasted_iota(jnp.int32, (LANES, LANES), 1) < HEAD_DIM)).astype(BF16)

    def head_sum(x):
        n = x.shape[0]
        tiles = jnp.concatenate([x[:, sl] for sl in sls], axis=0)
        sums = _dot(tiles.astype(BF16), ones_bd)
        return jnp.concatenate([sums[p * n:(p + 1) * n] for p in range(N_PAIRS)], axis=1)

    def bd(x):
        z = jnp.zeros_like(x)
        return jnp.concatenate([jnp.where(left, x, z), jnp.where(left, z, x)], axis=0)

    def pmm(a_, x_):
        return _dot(a_.astype(BF16), bd(x_.astype(BF16)))

    def assemble(tiles):
        return jnp.concatenate([jnp.concatenate(tiles[r * N_PAIRS:(r + 1) * N_PAIRS], axis=1)
                                for r in range(MIX_ROWS)], axis=0)

    def split(x):
        return [x[rw, sl] for rw, sl in chains]

    def solve_stage(stage, tm, pw):
        last = stage == N_STAGES
        prod = [_dot((tm[i] if last else jnp.concatenate([tm[i], pw[i]], axis=0)).astype(BF16),
                     bd(pw[i].astype(BF16))) for i in ids]
        tm = [tm[i] + prod[i][0:L] for i in ids]
        return tm, (None if last else [x[L:2 * L] for x in prod])

    at_o, rt_o = opsb_ref[rd, 0], opsb_ref[rd, 1]
    s_old = [s_ref[r][:, sl] for r in range(MIX_ROWS) for sl in sls]
    asrs = [lax.dot_general(jnp.concatenate([at_o[rw, sl], rt_o[rw, sl]], axis=0), bd(s_old[i].astype(BF16)),
                            _NT, preferred_element_type=F32) for i, (rw, sl) in enumerate(chains)]

    pooled = []
    for r in range(MIX_ROWS):
        u = u_ref[r].astype(F32)
        uext_ref[r, 0:POOL_PAD, :] = carry_ref[rd, r]
        uext_ref[r, POOL_PAD:POOL_PAD + L, :] = u
        carry_ref[wr, r] = u[L - POOL_PAD:L, :]
        run = u
        sums = {}
        for j in range(1, max(POOL_WINDOWS)):
            run = run + uext_ref[r, POOL_PAD - j:POOL_PAD - j + L, :]
            if j + 1 in POOL_WINDOWS:
                sums[j + 1] = run
        pos1 = pos0 + c * L + row_l + 1
        lane_u = lax.broadcasted_iota(jnp.int32, (1, D_POOL), 1)
        means = sums[POOL_WINDOWS[-1]] / jnp.minimum(pos1, POOL_WINDOWS[-1]).astype(F32)
        for gi in range(len(POOL_WINDOWS) - 2, -1, -1):
            w = POOL_WINDOWS[gi]
            means = jnp.where(lane_u < (gi + 1) * HEAD_DIM, sums[w] / jnp.minimum(pos1, w).astype(F32), means)
        pooled.append(means - u)
    pool_out = _dot(jnp.concatenate(pooled, axis=0).astype(BF16), pw_ref[...]) * ps_ref[...]

    zr = zr_ref[...].reshape(RL, D_SHIFT).astype(F32)
    last = slice(F32_SUBLANES - 1, F32_SUBLANES)
    rolled = pltpu.roll(zr, 1, 0)
    first = lax.broadcasted_iota(jnp.int32, (F32_SUBLANES, 1), 0) == 0
    parts = []
    for r in range(MIX_ROWS):
        parts += [jnp.where(first, prev_ref[rd, r, last, :], rolled[r * L:r * L + F32_SUBLANES]),
                  rolled[r * L + F32_SUBLANES:(r + 1) * L]]
    zprev = jnp.concatenate(parts, axis=0)
    for r in range(MIX_ROWS):
        prev_ref[wr, r] = zr[(r + 1) * L - F32_SUBLANES:(r + 1) * L, :]
    zs = zr + (zprev - zr) * mu_ref[...]
    r_ = zs[:, 0:D_RWKV]
    k = zs[:, D_RWKV:2 * D_RWKV]
    v = zs[:, 2 * D_RWKV:3 * D_RWKV]
    lwa = zs[:, 3 * D_RWKV:3 * D_RWKV + LANES]
    lg = zs[:, 3 * D_RWKV + LANES:]
    tl = jnp.where(left, jnp.tanh(lwa), lwa).astype(BF16)
    logw = -math.exp(-0.5) * _sigmoid(w0_ref[...] + _dot(tl, w2_ref[...]))
    a = _sigmoid(a0_ref[...] + _dot(tl, a2_ref[...]))
    g = _dot(_sigmoid(lg).astype(BF16), g2_ref[...])

    tm_o, pw_o = solve_stage(N_STAGES - 2, split(opsf_ref[rd, 0]), split(opsf_ref[rd, 1]))

    kkr = k * kk_ref[...]
    kk = kkr * jnp.minimum(jnp.exp(-0.5 * jnp.log(head_sum(kkr * kkr))), 1e12)
    k2 = k * (1.0 + (a - 1.0) * ka_ref[...])
    b = kk * a
    if t_valid < L:
        valid = rin < t_valid
        logw = jnp.where(valid, logw, 0.0)
        b = jnp.where(valid, b, 0.0)
        k2 = jnp.where(valid, k2, 0.0)
    ri = lax.broadcasted_iota(jnp.int32, (RL, RL), 0)
    ci = lax.broadcasted_iota(jnp.int32, (RL, RL), 1)
    same_row = (ri & -L) == (ci & -L)
    sel = jnp.concatenate([(ri >= ci) & same_row, same_row], axis=0).astype(BF16)
    hi = logw.astype(BF16)
    lo = (logw - hi.astype(F32)).astype(BF16)
    sums = _dot(sel, hi) + _dot(sel, lo)
    cum = sums[0:RL]
    cum_lb = sums[RL:]

    tm_o, pw_o = solve_stage(N_STAGES - 1, tm_o, pw_o)

    e_cum = jnp.exp(cum)
    e_inv = jnp.exp(-cum)
    e_prev = jnp.where(rin == 0, 1.0, pltpu.roll(e_cum, 1, 0))
    p_new = [jnp.exp(cum_lb[r * L:r * L + F32_SUBLANES, :]) for r in range(MIX_ROWS)]
    e_tail = jnp.concatenate([x for x in p_new for _ in range(L // F32_SUBLANES)], axis=0) * e_inv
    at = (-kk * e_prev).astype(BF16)
    rt = (r_ * e_cum).astype(BF16)
    bt = (b * e_inv).astype(BF16)
    kt = (k2 * e_inv).astype(BF16)
    vb = v.astype(BF16)
    sc = [lax.dot_general(jnp.concatenate([at[rw, sl], rt[rw, sl]], axis=0),
                          jnp.concatenate([bd(bt[rw, sl]), bd(kt[rw, sl])], axis=0),
                          _NT, preferred_element_type=F32) for rw, sl in chains]
    n_ = [jnp.where(strict, x[0:L, 0:2 * L], 0.0) for x in sc]
    wb = [jnp.where(incl, x[L:2 * L, 0:2 * L], 0.0) for x in sc]
    m_ = [jnp.where(strict, x[0:L, 2 * L:4 * L], 0.0) for x in sc]
    wk = [jnp.where(incl, x[L:2 * L, 2 * L:4 * L], 0.0) for x in sc]

    tm_o, _ = solve_stage(N_STAGES, tm_o, pw_o)

    v_n = split(vb)
    mwv = [_dot(jnp.concatenate([m_[i], wk[i]], axis=0).astype(BF16), bd(v_n[i])) for i in ids]
    tm = [eye + x for x in n_]
    pw = [pmm(x, x) for x in n_]

    mv_o, wkv_o = split(opsf_ref[rd, 2]), split(opsf_ref[rd, 3])
    u_b = [pmm(tm_o[i], asrs[i][0:L] + mv_o[i]).astype(BF16) for i in ids]

    bonus_v = head_sum(r_ * k2 * rk_ref[...]) * v
    tm, pw = solve_stage(2, tm, pw)

    wb_o = split(opsb_ref[rd, 5])
    ys = [asrs[i][L:2 * L] + wkv_o[i] + pmm(wb_o[i], u_b[i]) for i in ids]
    bh_o, kh_o, v_o = opsb_ref[rd, 2], opsb_ref[rd, 3], opsb_ref[rd, 4]
    full = [lax.dot_general(jnp.concatenate([u_b[i], v_o[rw, sl]], axis=0),
                            jnp.concatenate([bh_o[rw, sl], kh_o[rw, sl]], axis=0),
                            _TN, preferred_element_type=F32) for i, (rw, sl) in enumerate(chains)]
    for r in range(MIX_ROWS):
        p_l = pl_ref[rd, r, 0:1, :]
        s_ref[r] = jnp.concatenate(
            [s_old[r * N_PAIRS + p] * p_l[:, sl]
             + jnp.where(left, full[r * N_PAIRS + p][0:HEAD_DIM], full[r * N_PAIRS + p][HEAD_DIM:])
             for p, sl in enumerate(sls)], axis=1)
    y = assemble(ys)

    tm, pw = solve_stage(3, tm, pw)

    mean = head_sum(y) * (1.0 / HEAD_DIM)
    yc = y - mean
    var = head_sum(yc * yc) * (1.0 / HEAD_DIM)
    yn = yc * lax.rsqrt(var + LNX_EPS) * lng_ref[...] + lnb_ref[...]
    rwkv_out = (yn + opsf_ref[rd, 4]) * opsf_ref[rd, 5]
    out_ref[...] = jnp.concatenate([pool_ref[rd], rwkv_out], axis=1).astype(BF16).reshape(MIX_ROWS, L, D_MODEL)

    new_b = [at, rt, (b * e_tail).astype(BF16), (k2 * e_tail).astype(BF16), vb, assemble(wb).astype(BF16)]
    new_f = [assemble(tm), assemble(pw), assemble([x[0:L] for x in mwv]), assemble([x[L:2 * L] for x in mwv]),
             bonus_v, g]
    for i, x in enumerate(new_b):
        opsb_ref[wr, i] = x
    for i, x in enumerate(new_f):
        opsf_ref[wr, i] = x
    pool_ref[wr] = pool_out
    for r in range(MIX_ROWS):
        pl_ref[wr, r] = p_new[r]


def _const_spec(shape):
    return pl.BlockSpec(shape, lambda *_: (0,) * len(shape), pipeline_mode=pl.Buffered(1))


def _row_spec(width):
    return pl.BlockSpec((TOKEN_TILE, width), lambda i: (i, 0))


def _ffn1_call(x, lw):
    n = x.shape[0]
    consts = (lw["ffn1_w_in"], lw["ffn1_w_out"], lw["ln1_g"], lw["ln1_b"], lw["w_in"])
    return pl.pallas_call(
        _ffn1_kernel,
        grid=(n // TOKEN_TILE,),
        in_specs=[_row_spec(D_MODEL)] + [_const_spec(c.shape) for c in consts],
        out_specs=[_row_spec(D_MODEL), _row_spec(D_POOL), _row_spec(D_SHIFT)],
        out_shape=[jax.ShapeDtypeStruct((n, D_MODEL), F32),
                   jax.ShapeDtypeStruct((n, D_POOL), BF16),
                   jax.ShapeDtypeStruct((n, D_SHIFT), BF16)],
        compiler_params=pltpu.CompilerParams(dimension_semantics=("parallel",),
                                             vmem_limit_bytes=DENSE_VMEM_LIMIT),
        name="ffn1",
    )(x, *consts)


def _post_call(mix, x1, p, lw):
    n = x1.shape[0]
    consts = (lw["w_out"], lw["ln2_g"], lw["ln2_b"], lw["ffn2_w_in"], lw["ffn2_w_out"],
              lw["ple_gate_w"], lw["ple_gate_b"], lw["ple_w"], lw["ln3_g"], lw["ln3_b"])
    return pl.pallas_call(
        _post_kernel,
        grid=(n // TOKEN_TILE,),
        in_specs=[_row_spec(D_MODEL), _row_spec(D_MODEL), _row_spec(D_PLE)]
        + [_const_spec(c.shape) for c in consts],
        out_specs=_row_spec(D_MODEL),
        out_shape=jax.ShapeDtypeStruct((n, D_MODEL), F32),
        compiler_params=pltpu.CompilerParams(dimension_semantics=("parallel",),
                                             vmem_limit_bytes=DENSE_VMEM_LIMIT),
        name="post",
    )(mix, x1, p, *consts)


def _mix_call(u, zr, hist, shift, s0, lw, t_valid, pos0):
    bn, tp, _ = u.shape
    nc = tp // CHUNK
    consts = (lw["mu_shift"], lw["w0"], lw["w2"], lw["a0"], lw["a2"], lw["g2"], lw["k_k"], lw["k_a"],
              lw["r_k"], lw["lnx_g"], lw["lnx_b"], lw["pool_w"], lw["pool_scale"])
    rl = MIX_ROWS * CHUNK
    per_row = lambda shape: pl.BlockSpec((MIX_ROWS,) + shape, lambda bi, ci: (bi, 0, 0))
    chunk_in = lambda width: pl.BlockSpec((MIX_ROWS, CHUNK, width), lambda bi, ci: (bi, jnp.minimum(ci, nc - 1), 0))
    chunk_out = lambda width: pl.BlockSpec((MIX_ROWS, CHUNK, width), lambda bi, ci: (bi, jnp.maximum(ci - 1, 0), 0))
    return pl.pallas_call(
        functools.partial(_mix_kernel, t_valid, pos0),
        grid=(bn // MIX_ROWS, nc + 1),
        in_specs=[chunk_in(D_POOL), chunk_in(D_SHIFT), per_row((POOL_PAD, D_POOL)),
                  per_row((1, D_SHIFT)), per_row((HEAD_DIM, D_RWKV))]
        + [_const_spec(c.shape) for c in consts],
        out_specs=[chunk_out(D_MODEL), per_row((HEAD_DIM, D_RWKV))],
        out_shape=[jax.ShapeDtypeStruct((bn, tp, D_MODEL), BF16),
                   jax.ShapeDtypeStruct((bn, HEAD_DIM, D_RWKV), F32)],
        scratch_shapes=[pltpu.VMEM((MIX_ROWS, POOL_PAD + CHUNK, D_POOL), F32),
                        pltpu.VMEM((2, MIX_ROWS, POOL_PAD, D_POOL), F32),
                        pltpu.VMEM((2, MIX_ROWS, F32_SUBLANES, D_SHIFT), F32),
                        pltpu.VMEM((2, 6, rl, D_RWKV), BF16),
                        pltpu.VMEM((2, 6, rl, D_RWKV), F32),
                        pltpu.VMEM((2, rl, D_POOL), F32),
                        pltpu.VMEM((2, MIX_ROWS, F32_SUBLANES, D_RWKV), F32)],
        compiler_params=pltpu.CompilerParams(dimension_semantics=("parallel", "arbitrary")),
        name="mix",
    )(u, zr, hist, shift, s0, *consts)


def _prepare_layer(i, W):
    row = lambda name: W[name][i].reshape(1, -1).astype(F32)
    mat = lambda name: W[name][i].astype(BF16)
    zeros = jnp.zeros((HEAD_DIM, D_RWKV), BF16)
    lw = {name: mat(name) for name in ("ffn1_w_in", "ffn1_w_out", "ffn2_w_in", "ffn2_w_out", "w_in", "g2",
                                       "w_out", "ple_gate_w", "ple_w")}
    lw.update({name: row(name) for name in ("ln1_g", "ln1_b", "ln2_g", "ln2_b", "ln3_g", "ln3_b", "mu_shift",
                                            "pool_scale", "w0", "a0", "k_k", "k_a", "r_k", "lnx_g", "lnx_b",
                                            "ple_gate_b")})
    lw["w2"] = jnp.concatenate([mat("w2"), zeros], axis=0)
    lw["a2"] = jnp.concatenate([zeros, mat("a2")], axis=0)
    lw["pool_w"] = jax.scipy.linalg.block_diag(*[W["pool_w"][i, gi] for gi in range(len(POOL_WINDOWS))]).astype(BF16)
    return lw


def _run_trunk(x, p, pos0, pool_hist, shift_prev, wkv_prev, layers):
    bn, t, _ = x.shape
    n = bn * t
    tp = -(-t // CHUNK) * CHUNK
    assert tp == t or t < CHUNK, "time padding is only supported for a single partial chunk"
    xf = x.reshape(n, D_MODEL)
    pools, shifts, wkvs = [], [], []
    for i, lw in enumerate(layers):
        x1, u, zr = _ffn1_call(xf, lw)
        u = u.reshape(bn, t, D_POOL)
        zr = zr.reshape(bn, t, D_SHIFT)
        hist = jnp.zeros((bn, POOL_HIST, D_POOL), F32) if pool_hist is None else pool_hist[i]
        pools.append(jnp.concatenate([hist, u[:, -POOL_HIST:]], axis=1)[:, -POOL_HIST:])
        shifts.append(zr[:, -1].astype(F32))
        if tp != t:
            u = jnp.pad(u, ((0, 0), (0, tp - t), (0, 0)))
            zr = jnp.pad(zr, ((0, 0), (0, tp - t), (0, 0)))
        hist = jnp.pad(hist, ((0, 0), (POOL_PAD - POOL_HIST, 0), (0, 0)))
        s0 = jnp.transpose(wkv_prev[i], (0, 2, 1, 3)).reshape(bn, HEAD_DIM, D_RWKV)
        mix, s_new = _mix_call(u, zr, hist, shift_prev[i][:, None, :], s0, lw, min(t, CHUNK), pos0)
        wkvs.append(jnp.transpose(s_new.reshape(bn, HEAD_DIM, N_HEADS, HEAD_DIM), (0, 2, 1, 3)))
        xf = _post_call(mix[:, :t].reshape(n, D_MODEL), x1, p[i].reshape(n, D_PLE), lw)
    return xf.reshape(bn, t, D_MODEL), jnp.stack(pools), jnp.stack(shifts), jnp.stack(wkvs)


def kernel(x_prompt, x_sample, p_prompt, p_sample, state_pool, state_shift, state_wkv, ln1_g, ln1_b, ln2_g, ln2_b, ln3_g, ln3_b, ffn1_w_in, ffn1_w_out, ffn2_w_in, ffn2_w_out, w_in, mu_shift, pool_w, pool_scale, w0, w2, a0, a2, g2, k_k, k_a, r_k, lnx_g, lnx_b, w_out, ple_gate_w, ple_gate_b, ple_w):
    W = dict(ln1_g=ln1_g, ln1_b=ln1_b, ln2_g=ln2_g, ln2_b=ln2_b, ln3_g=ln3_g, ln3_b=ln3_b,
             ffn1_w_in=ffn1_w_in, ffn1_w_out=ffn1_w_out, ffn2_w_in=ffn2_w_in, ffn2_w_out=ffn2_w_out,
             w_in=w_in, mu_shift=mu_shift, pool_w=pool_w, pool_scale=pool_scale, w0=w0, w2=w2,
             a0=a0, a2=a2, g2=g2, k_k=k_k, k_a=k_a, r_k=r_k, lnx_g=lnx_g, lnx_b=lnx_b,
             w_out=w_out, ple_gate_w=ple_gate_w, ple_gate_b=ple_gate_b, ple_w=ple_w)
    layers = [_prepare_layer(i, W) for i in range(DEPTH)]
    bn = x_prompt.shape[0]
    y_prompt, pool_prompt, shift_prompt, wkv_prompt = _run_trunk(
        x_prompt, p_prompt, 0, None,
        jnp.zeros((DEPTH, bn, D_SHIFT), F32),
        jnp.zeros((DEPTH, bn, N_HEADS, HEAD_DIM, HEAD_DIM), F32), layers)
    y_sample, pool_sample, shift_sample, wkv_sample = _run_trunk(
        x_sample, p_sample, PAST_LEN, state_pool, state_shift, state_wkv, layers)
    return (y_prompt, y_sample, pool_prompt, shift_prompt, wkv_prompt, pool_sample, shift_sample, wkv_sample)
```

```python
import functools
import math

import jax
import jax.numpy as jnp
from jax import lax
from jax.experimental import pallas as pl
from jax.experimental.pallas import tpu as pltpu

D_MODEL = 1024
D_POOL = 256
D_RWKV = 768
HEAD_DIM = 64
N_HEADS = 12
LANES = 128
N_PAIRS = D_RWKV // LANES
D_SHIFT = 2560
D_FF = 2816
D_PLE = 256
POOL_WINDOWS = (2, 4, 8, 16)
POOL_HIST = 15
POOL_PAD = 16
DEPTH = 2
ALPHA = (2.0 * DEPTH) ** 0.25
LN_EPS = 1e-5
LNX_EPS = 64e-5
PAST_LEN = 1024
CHUNK = 64
N_STAGES = int(math.log2(CHUNK))
TOKEN_TILE = 512
SUB_TILE = TOKEN_TILE // 2
F32_SUBLANES = 8
MIX_ROWS = 2
V7X_VMEM_BYTES = 64 * 1024 * 1024
DENSE_VMEM_LIMIT = V7X_VMEM_BYTES * 7 // 8

F32 = jnp.float32
BF16 = jnp.bfloat16
_NT = (((1,), (1,)), ((), ()))
_TN = (((0,), (0,)), ((), ()))


def _dot(a, b):
    return jnp.dot(a, b, preferred_element_type=F32)


def _sigmoid(x):
    return 0.5 * jnp.tanh(0.5 * x) + 0.5


def _layer_norm(h, g, b, eps):
    mu = jnp.mean(h, axis=-1, keepdims=True)
    d = h - mu
    var = jnp.mean(d * d, axis=-1, keepdims=True)
    return d * lax.rsqrt(var + eps) * g + b


def _swiglu_halves(xbs, w_in_ref, w_out_ref):
    hs = [(_dot(xb, w_in_ref[:, :D_FF]), _dot(xb, w_in_ref[:, D_FF:])) for xb in xbs]
    return [_dot((gate * _sigmoid(gate) * up).astype(BF16), w_out_ref[...]) for gate, up in hs]


def _halves(ref):
    return [ref[0:SUB_TILE, :], ref[SUB_TILE:TOKEN_TILE, :]]


def _ffn1_kernel(x_ref, w_in_ref, w_out_ref, g_ref, b_ref, wz_ref, x1_ref, u_ref, zr_ref):
    xs = _halves(x_ref)
    ffn = _swiglu_halves([x.astype(BF16) for x in xs], w_in_ref, w_out_ref)
    for i, (x, f) in enumerate(zip(xs, ffn)):
        rows = slice(i * SUB_TILE, (i + 1) * SUB_TILE)
        x1 = _layer_norm(ALPHA * x + 0.5 * f, g_ref[...], b_ref[...], LN_EPS)
        x1_ref[rows, :] = x1
        z = _dot(x1.astype(BF16), wz_ref[...]).astype(BF16)
        u_ref[rows, :] = z[:, :D_POOL]
        zr_ref[rows, :] = z[:, D_POOL:]


def _post_kernel(mix_ref, x1_ref, p_ref, wo_ref, g2_ref, b2_ref, w_in_ref, w_out_ref,
                 wg_ref, bg_ref, wp_ref, g3_ref, b3_ref, out_ref):
    mixes = [_dot(m, wo_ref[...]) for m in _halves(mix_ref)]
    x2s = [_layer_norm(ALPHA * x1 + mix, g2_ref[...], b2_ref[...], LN_EPS) for x1, mix in zip(_halves(x1_ref), mixes)]
    ffn = _swiglu_halves([x2.astype(BF16) for x2 in x2s], w_in_ref, w_out_ref)
    ples = [_dot(p.astype(BF16), wp_ref[...]) for p in _halves(p_ref)]
    for i, (x2, f, ple) in enumerate(zip(x2s, ffn, ples)):
        h = ALPHA * x2 + 0.5 * f
        gate = _sigmoid(_dot(h.astype(BF16), wg_ref[...]) + bg_ref[...])
        out_ref[i * SUB_TILE:(i + 1) * SUB_TILE, :] = _layer_norm(h + gate * ple, g3_ref[...], b3_ref[...], LN_EPS)


def _mix_kernel(t_valid, pos0,
                u_ref, zr_ref, hist_ref, shift_ref, s0_ref,
                mu_ref, w0_ref, w2_ref, a0_ref, a2_ref, g2_ref, kk_ref, ka_ref, rk_ref,
                lng_ref, lnb_ref, pw_ref, ps_ref,
                out_ref, s_ref,
                uext_ref, carry_ref, prev_ref, opsb_ref, opsf_ref, pool_ref, pl_ref):
    c = pl.program_id(1)
    refs = (u_ref, zr_ref, mu_ref, w0_ref, w2_ref, a0_ref, a2_ref, g2_ref, kk_ref, ka_ref, rk_ref,
            lng_ref, lnb_ref, pw_ref, ps_ref, out_ref, s_ref,
            uext_ref, carry_ref, prev_ref, opsb_ref, opsf_ref, pool_ref, pl_ref)

    @pl.when(c == 0)
    def _():
        carry_ref[1] = hist_ref[...]
        prev_ref[1, :, F32_SUBLANES - 1:F32_SUBLANES, :] = shift_ref[...]
        s_ref[...] = s0_ref[...]
        opsb_ref[1] = jnp.zeros(opsb_ref.shape[1:], BF16)
        opsf_ref[1] = jnp.zeros(opsf_ref.shape[1:], F32)
        pool_ref[1] = jnp.zeros(pool_ref.shape[1:], F32)
        pl_ref[1] = jnp.ones(pl_ref.shape[1:], F32)

    @pl.when((c & 1) == 0)
    def _():
        _mix_step(t_valid, pos0, 1, 0, c, *refs)

    @pl.when((c & 1) == 1)
    def _():
        _mix_step(t_valid, pos0, 0, 1, c, *refs)


def _mix_step(t_valid, pos0, rd, wr, c,
              u_ref, zr_ref, mu_ref, w0_ref, w2_ref, a0_ref, a2_ref, g2_ref, kk_ref, ka_ref, rk_ref,
              lng_ref, lnb_ref, pw_ref, ps_ref, out_ref, s_ref,
              uext_ref, carry_ref, prev_ref, opsb_ref, opsf_ref, pool_ref, pl_ref):
    assert N_STAGES == 6, "the prepare/apply split below places solve stages 2-3 and 4-6 explicitly"
    L = CHUNK
    RL = MIX_ROWS * L
    row_l = lax.broadcasted_iota(jnp.int32, (L, 1), 0)
    row = lax.broadcasted_iota(jnp.int32, (RL, 1), 0)
    rin = row & (L - 1)
    lane = lax.broadcasted_iota(jnp.int32, (1, LANES), 1)
    left = lane < HEAD_DIM
    col = lane & (HEAD_DIM - 1)
    strict = col < row_l
    incl = col <= row_l
    eye = (col == row_l).astype(F32)
    sls = [slice(p * LANES, (p + 1) * LANES) for p in range(N_PAIRS)]
    rws = [slice(r * L, (r + 1) * L) for r in range(MIX_ROWS)]
    chains = [(rw, sl) for rw in rws for sl in sls]
    ids = range(len(chains))

    ones_bd = ((lax.broadcasted_iota(jnp.int32, (LANES, LANES), 0) < HEAD_DIM)
               == (lax.broadcasted_iota(jnp.int32, (LANES, LANES), 1) < HEAD_DIM)).astype(BF16)

    def head_sum(x):
        n = x.shape[0]
        tiles = jnp.concatenate([x[:, sl] for sl in sls], axis=0)
        sums = _dot(tiles.astype(BF16), ones_bd)
        return jnp.concatenate([sums[p * n:(p + 1) * n] for p in range(N_PAIRS)], axis=1)

    def bd(x):
        z = jnp.zeros_like(x)
        return jnp.concatenate([jnp.where(left, x, z), jnp.where(left, z, x)], axis=0)

    def pmm(a_, x_):
        return _dot(a_.astype(BF16), bd(x_.astype(BF16)))

    def assemble(tiles):
        return jnp.concatenate([jnp.concatenate(tiles[r * N_PAIRS:(r + 1) * N_PAIRS], axis=1)
                                for r in range(MIX_ROWS)], axis=0)

    def split(x):
        return [x[rw, sl] for rw, sl in chains]

    def solve_stage(stage, tm, pw):
        last = stage == N_STAGES
        prod = [_dot((tm[i] if last else jnp.concatenate([tm[i], pw[i]], axis=0)).astype(BF16),
                     bd(pw[i].astype(BF16))) for i in ids]
        tm = [tm[i] + prod[i][0:L] for i in ids]
        return tm, (None if last else [x[L:2 * L] for x in prod])

    at_o, rt_o = opsb_ref[rd, 0], opsb_ref[rd, 1]
    s_old = [s_ref[r][:, sl] for r in range(MIX_ROWS) for sl in sls]
    asrs = [lax.dot_general(jnp.concatenate([at_o[rw, sl], rt_o[rw, sl]], axis=0), bd(s_old[i].astype(BF16)),
                            _NT, preferred_element_type=F32) for i, (rw, sl) in enumerate(chains)]

    pooled = []
    for r in range(MIX_ROWS):
        u = u_ref[r].astype(F32)
        uext_ref[r, 0:POOL_PAD, :] = carry_ref[rd, r]
        uext_ref[r, POOL_PAD:POOL_PAD + L, :] = u
        carry_ref[wr, r] = u[L - POOL_PAD:L, :]
        run = u
        sums = {}
        for j in range(1, max(POOL_WINDOWS)):
            run = run + uext_ref[r, POOL_PAD - j:POOL_PAD - j + L, :]
            if j + 1 in POOL_WINDOWS:
                sums[j + 1] = run
        pos1 = pos0 + c * L + row_l + 1
        lane_u = lax.broadcasted_iota(jnp.int32, (1, D_POOL), 1)
        means = sums[POOL_WINDOWS[-1]] / jnp.minimum(pos1, POOL_WINDOWS[-1]).astype(F32)
        for gi in range(len(POOL_WINDOWS) - 2, -1, -1):
            w = POOL_WINDOWS[gi]
            means = jnp.where(lane_u < (gi + 1) * HEAD_DIM, sums[w] / jnp.minimum(pos1, w).astype(F32), means)
        pooled.append(means - u)
    pool_out = _dot(jnp.concatenate(pooled, axis=0).astype(BF16), pw_ref[...]) * ps_ref[...]

    zr = zr_ref[...].reshape(RL, D_SHIFT).astype(F32)
    last = slice(F32_SUBLANES - 1, F32_SUBLANES)
    rolled = pltpu.roll(zr, 1, 0)
    first = lax.broadcasted_iota(jnp.int32, (F32_SUBLANES, 1), 0) == 0
    parts = []
    for r in range(MIX_ROWS):
        parts += [jnp.where(first, prev_ref[rd, r, last, :], rolled[r * L:r * L + F32_SUBLANES]),
                  rolled[r * L + F32_SUBLANES:(r + 1) * L]]
    zprev = jnp.concatenate(parts, axis=0)
    for r in range(MIX_ROWS):
        prev_ref[wr, r] = zr[(r + 1) * L - F32_SUBLANES:(r + 1) * L, :]
    zs = zr + (zprev - zr) * mu_ref[...]
    r_ = zs[:, 0:D_RWKV]
    k = zs[:, D_RWKV:2 * D_RWKV]
    v = zs[:, 2 * D_RWKV:3 * D_RWKV]
    lwa = zs[:, 3 * D_RWKV:3 * D_RWKV + LANES]
    lg = zs[:, 3 * D_RWKV + LANES:]
    tl = jnp.where(left, jnp.tanh(lwa), lwa).astype(BF16)
    logw = -math.exp(-0.5) * _sigmoid(w0_ref[...] + _dot(tl, w2_ref[...]))
    a = _sigmoid(a0_ref[...] + _dot(tl, a2_ref[...]))
    g = _dot(_sigmoid(lg).astype(BF16), g2_ref[...])

    tm_o, pw_o = solve_stage(N_STAGES - 2, split(opsf_ref[rd, 0]), split(opsf_ref[rd, 1]))

    kkr = k * kk_ref[...]
    kk = kkr * jnp.minimum(jnp.exp(-0.5 * jnp.log(head_sum(kkr * kkr))), 1e12)
    k2 = k * (1.0 + (a - 1.0) * ka_ref[...])
    b = kk * a
    if t_valid < L:
        valid = rin < t_valid
        logw = jnp.where(valid, logw, 0.0)
        b = jnp.where(valid, b, 0.0)
        k2 = jnp.where(valid, k2, 0.0)
    ri = lax.broadcasted_iota(jnp.int32, (RL, RL), 0)
    ci = lax.broadcasted_iota(jnp.int32, (RL, RL), 1)
    same_row = (ri & -L) == (ci & -L)
    sel = jnp.concatenate([(ri >= ci) & same_row, same_row], axis=0).astype(BF16)
    hi = logw.astype(BF16)
    lo = (logw - hi.astype(F32)).astype(BF16)
    sums = _dot(sel, hi) + _dot(sel, lo)
    cum = sums[0:RL]
    cum_lb = sums[RL:]

    tm_o, pw_o = solve_stage(N_STAGES - 1, tm_o, pw_o)

    e_cum = jnp.exp(cum)
    e_inv = jnp.exp(-cum)
    e_prev = jnp.where(rin == 0, 1.0, pltpu.roll(e_cum, 1, 0))
    p_new = [jnp.exp(cum_lb[r * L:r * L + F32_SUBLANES, :]) for r in range(MIX_ROWS)]
    e_tail = jnp.concatenate([x for x in p_new for _ in range(L // F32_SUBLANES)], axis=0) * e_inv
    at = (-kk * e_prev).astype(BF16)
    rt = (r_ * e_cum).astype(BF16)
    bt = (b * e_inv).astype(BF16)
    kt = (k2 * e_inv).astype(BF16)
    vb = v.astype(BF16)
    sc = [lax.dot_general(jnp.concatenate([at[rw, sl], rt[rw, sl]], axis=0),
                          jnp.concatenate([bd(bt[rw, sl]), bd(kt[rw, sl])], axis=0),
                          _NT, preferred_element_type=F32) for rw, sl in chains]
    n_ = [jnp.where(strict, x[0:L, 0:2 * L], 0.0) for x in sc]
    wb = [jnp.where(incl, x[L:2 * L, 0:2 * L], 0.0) for x in sc]
    m_ = [jnp.where(strict, x[0:L, 2 * L:4 * L], 0.0) for x in sc]
    wk = [jnp.where(incl, x[L:2 * L, 2 * L:4 * L], 0.0) for x in sc]

    tm_o, _ = solve_stage(N_STAGES, tm_o, pw_o)

    v_n = split(vb)
    mwv = [_dot(jnp.concatenate([m_[i], wk[i]], axis=0).astype(BF16), bd(v_n[i])) for i in ids]
    tm = [eye + x for x in n_]
    pw = [pmm(x, x) for x in n_]

    mv_o, wkv_o = split(opsf_ref[rd, 2]), split(opsf_ref[rd, 3])
    u_b = [pmm(tm_o[i], asrs[i][0:L] + mv_o[i]).astype(BF16) for i in ids]

    bonus_v = head_sum(r_ * k2 * rk_ref[...]) * v
    tm, pw = solve_stage(2, tm, pw)

    wb_o = split(opsb_ref[rd, 5])
    ys = [asrs[i][L:2 * L] + wkv_o[i] + pmm(wb_o[i], u_b[i]) for i in ids]
    bh_o, kh_o, v_o = opsb_ref[rd, 2], opsb_ref[rd, 3], opsb_ref[rd, 4]
    full = [lax.dot_general(jnp.concatenate([u_b[i], v_o[rw, sl]], axis=0),
                            jnp.concatenate([bh_o[rw, sl], kh_o[rw, sl]], axis=0),
                            _TN, preferred_element_type=F32) for i, (rw, sl) in enumerate(chains)]
    for r in range(MIX_ROWS):
        p_l = pl_ref[rd, r, 0:1, :]
        s_ref[r] = jnp.concatenate(
            [s_old[r * N_PAIRS + p] * p_l[:, sl]
             + jnp.where(left, full[r * N_PAIRS + p][0:HEAD_DIM], full[r * N_PAIRS + p][HEAD_DIM:])
             for p, sl in enumerate(sls)], axis=1)
    y = assemble(ys)

    tm, pw = solve_stage(3, tm, pw)

    mean = head_sum(y) * (1.0 / HEAD_DIM)
    yc = y - mean
    var = head_sum(yc * yc) * (1.0 / HEAD_DIM)
    yn = yc * lax.rsqrt(var + LNX_EPS) * lng_ref[...] + lnb_ref[...]
    rwkv_out = (yn + opsf_ref[rd, 4]) * opsf_ref[rd, 5]
    out_ref[...] = jnp.concatenate([pool_ref[rd], rwkv_out], axis=1).astype(BF16).reshape(MIX_ROWS, L, D_MODEL)

    new_b = [at, rt, (b * e_tail).astype(BF16), (k2 * e_tail).astype(BF16), vb, assemble(wb).astype(BF16)]
    new_f = [assemble(tm), assemble(pw), assemble([x[0:L] for x in mwv]), assemble([x[L:2 * L] for x in mwv]),
             bonus_v, g]
    for i, x in enumerate(new_b):
        opsb_ref[wr, i] = x
    for i, x in enumerate(new_f):
        opsf_ref[wr, i] = x
    pool_ref[wr] = pool_out
    for r in range(MIX_ROWS):
        pl_ref[wr, r] = p_new[r]


def _const_spec(shape):
    return pl.BlockSpec(shape, lambda *_: (0,) * len(shape), pipeline_mode=pl.Buffered(1))


def _row_spec(width):
    return pl.BlockSpec((TOKEN_TILE, width), lambda i: (i, 0))


def _ffn1_call(x, lw):
    n = x.shape[0]
    consts = (lw["ffn1_w_in"], lw["ffn1_w_out"], lw["ln1_g"], lw["ln1_b"], lw["w_in"])
    return pl.pallas_call(
        _ffn1_kernel,
        grid=(n // TOKEN_TILE,),
        in_specs=[_row_spec(D_MODEL)] + [_const_spec(c.shape) for c in consts],
        out_specs=[_row_spec(D_MODEL), _row_spec(D_POOL), _row_spec(D_SHIFT)],
        out_shape=[jax.ShapeDtypeStruct((n, D_MODEL), F32),
                   jax.ShapeDtypeStruct((n, D_POOL), BF16),
                   jax.ShapeDtypeStruct((n, D_SHIFT), BF16)],
        compiler_params=pltpu.CompilerParams(dimension_semantics=("parallel",),
                                             vmem_limit_bytes=DENSE_VMEM_LIMIT),
        name="ffn1",
    )(x, *consts)


def _post_call(mix, x1, p, layer, lw):
    n = x1.shape[0]
    consts = (lw["w_out"], lw["ln2_g"], lw["ln2_b"], lw["ffn2_w_in"], lw["ffn2_w_out"],
              lw["ple_gate_w"], lw["ple_gate_b"], lw["ple_w"], lw["ln3_g"], lw["ln3_b"])
    return pl.pallas_call(
        _post_kernel,
        grid=(n // TOKEN_TILE,),
        in_specs=[_row_spec(D_MODEL), _row_spec(D_MODEL),
                  pl.BlockSpec((None, TOKEN_TILE, D_PLE), lambda i: (layer, i, 0))]
        + [_const_spec(c.shape) for c in consts],
        out_specs=_row_spec(D_MODEL),
        out_shape=jax.ShapeDtypeStruct((n, D_MODEL), F32),
        compiler_params=pltpu.CompilerParams(dimension_semantics=("parallel",),
                                             vmem_limit_bytes=DENSE_VMEM_LIMIT),
        name="post",
    )(mix, x1, p, *consts)


def _mix_call(u, zr, hist, shift, s0, lw, t_valid, pos0):
    bn, tp, _ = u.shape
    nc = tp // CHUNK
    consts = (lw["mu_shift"], lw["w0"], lw["w2"], lw["a0"], lw["a2"], lw["g2"], lw["k_k"], lw["k_a"],
              lw["r_k"], lw["lnx_g"], lw["lnx_b"], lw["pool_w"], lw["pool_scale"])
    rl = MIX_ROWS * CHUNK
    per_row = lambda shape: pl.BlockSpec((MIX_ROWS,) + shape, lambda bi, ci: (bi, 0, 0))
    chunk_in = lambda width: pl.BlockSpec((MIX_ROWS, CHUNK, width), lambda bi, ci: (bi, jnp.minimum(ci, nc - 1), 0))
    chunk_out = lambda width: pl.BlockSpec((MIX_ROWS, CHUNK, width), lambda bi, ci: (bi, jnp.maximum(ci - 1, 0), 0))
    return pl.pallas_call(
        functools.partial(_mix_kernel, t_valid, pos0),
        grid=(bn // MIX_ROWS, nc + 1),
        in_specs=[chunk_in(D_POOL), chunk_in(D_SHIFT), per_row((POOL_PAD, D_POOL)),
                  per_row((1, D_SHIFT)), per_row((HEAD_DIM, D_RWKV))]
        + [_const_spec(c.shape) for c in consts],
        out_specs=[chunk_out(D_MODEL), per_row((HEAD_DIM, D_RWKV))],
        out_shape=[jax.ShapeDtypeStruct((bn, tp, D_MODEL), BF16),
                   jax.ShapeDtypeStruct((bn, HEAD_DIM, D_RWKV), F32)],
        scratch_shapes=[pltpu.VMEM((MIX_ROWS, POOL_PAD + CHUNK, D_POOL), F32),
                        pltpu.VMEM((2, MIX_ROWS, POOL_PAD, D_POOL), F32),
                        pltpu.VMEM((2, MIX_ROWS, F32_SUBLANES, D_SHIFT), F32),
                        pltpu.VMEM((2, 6, rl, D_RWKV), BF16),
                        pltpu.VMEM((2, 6, rl, D_RWKV), F32),
                        pltpu.VMEM((2, rl, D_POOL), F32),
                        pltpu.VMEM((2, MIX_ROWS, F32_SUBLANES, D_RWKV), F32)],
        compiler_params=pltpu.CompilerParams(dimension_semantics=("parallel", "arbitrary")),
        name="mix",
    )(u, zr, hist, shift, s0, *consts)


def _prepare_layer(i, W):
    row = lambda name: W[name][i].reshape(1, -1).astype(F32)
    mat = lambda name: W[name][i].astype(BF16)
    zeros = jnp.zeros((HEAD_DIM, D_RWKV), BF16)
    lw = {name: mat(name) for name in ("ffn1_w_in", "ffn1_w_out", "ffn2_w_in", "ffn2_w_out", "w_in", "g2",
                                       "w_out", "ple_gate_w", "ple_w")}
    lw.update({name: row(name) for name in ("ln1_g", "ln1_b", "ln2_g", "ln2_b", "ln3_g", "ln3_b", "mu_shift",
                                            "pool_scale", "w0", "a0", "k_k", "k_a", "r_k", "lnx_g", "lnx_b",
                                            "ple_gate_b")})
    lw["w2"] = jnp.concatenate([mat("w2"), zeros], axis=0)
    lw["a2"] = jnp.concatenate([zeros, mat("a2")], axis=0)
    lw["pool_w"] = jax.scipy.linalg.block_diag(*[W["pool_w"][i, gi] for gi in range(len(POOL_WINDOWS))]).astype(BF16)
    return lw


def _run_trunk(x, p, pos0, pool_hist, shift_prev, wkv_prev, layers):
    bn, t, _ = x.shape
    n = bn * t
    tp = -(-t // CHUNK) * CHUNK
    assert tp == t or t < CHUNK, "time padding is only supported for a single partial chunk"
    xf = x.reshape(n, D_MODEL)
    pools, shifts, wkvs = [], [], []
    for i, lw in enumerate(layers):
        x1, u, zr = _ffn1_call(xf, lw)
        u = u.reshape(bn, t, D_POOL)
        zr = zr.reshape(bn, t, D_SHIFT)
        hist = jnp.zeros((bn, POOL_HIST, D_POOL), F32) if pool_hist is None else pool_hist[i]
        pools.append(jnp.concatenate([hist, u[:, -POOL_HIST:]], axis=1)[:, -POOL_HIST:])
        shifts.append(zr[:, -1].astype(F32))
        if tp != t:
            u = jnp.pad(u, ((0, 0), (0, tp - t), (0, 0)))
            zr = jnp.pad(zr, ((0, 0), (0, tp - t), (0, 0)))
        hist = jnp.pad(hist, ((0, 0), (POOL_PAD - POOL_HIST, 0), (0, 0)))
        s0 = jnp.transpose(wkv_prev[i], (0, 2, 1, 3)).reshape(bn, HEAD_DIM, D_RWKV)
        mix, s_new = _mix_call(u, zr, hist, shift_prev[i][:, None, :], s0, lw, min(t, CHUNK), pos0)
        wkvs.append(jnp.transpose(s_new.reshape(bn, HEAD_DIM, N_HEADS, HEAD_DIM), (0, 2, 1, 3)))
        xf = _post_call(mix[:, :t].reshape(n, D_MODEL), x1, p.reshape(DEPTH, n, D_PLE), i, lw)
    return xf.reshape(bn, t, D_MODEL), jnp.stack(pools), jnp.stack(shifts), jnp.stack(wkvs)


def kernel(x_prompt, x_sample, p_prompt, p_sample, state_pool, state_shift, state_wkv, ln1_g, ln1_b, ln2_g, ln2_b, ln3_g, ln3_b, ffn1_w_in, ffn1_w_out, ffn2_w_in, ffn2_w_out, w_in, mu_shift, pool_w, pool_scale, w0, w2, a0, a2, g2, k_k, k_a, r_k, lnx_g, lnx_b, w_out, ple_gate_w, ple_gate_b, ple_w):
    W = dict(ln1_g=ln1_g, ln1_b=ln1_b, ln2_g=ln2_g, ln2_b=ln2_b, ln3_g=ln3_g, ln3_b=ln3_b,
             ffn1_w_in=ffn1_w_in, ffn1_w_out=ffn1_w_out, ffn2_w_in=ffn2_w_in, ffn2_w_out=ffn2_w_out,
             w_in=w_in, mu_shift=mu_shift, pool_w=pool_w, pool_scale=pool_scale, w0=w0, w2=w2,
             a0=a0, a2=a2, g2=g2, k_k=k_k, k_a=k_a, r_k=r_k, lnx_g=lnx_g, lnx_b=lnx_b,
             w_out=w_out, ple_gate_w=ple_gate_w, ple_gate_b=ple_gate_b, ple_w=ple_w)
    layers = [_prepare_layer(i, W) for i in range(DEPTH)]
    bn = x_prompt.shape[0]
    y_prompt, pool_prompt, shift_prompt, wkv_prompt = _run_trunk(
        x_prompt, p_prompt, 0, None,
        jnp.zeros((DEPTH, bn, D_SHIFT), F32),
        jnp.zeros((DEPTH, bn, N_HEADS, HEAD_DIM, HEAD_DIM), F32), layers)
    y_sample, pool_sample, shift_sample, wkv_sample = _run_trunk(
        x_sample, p_sample, PAST_LEN, state_pool, state_shift, state_wkv, layers)
    return (y_prompt, y_sample, pool_prompt, shift_prompt, wkv_prompt, pool_sample, shift_sample, wkv_sample)
```

```python
import functools
import math

import jax
import jax.numpy as jnp
from jax import lax
from jax.experimental import pallas as pl
from jax.experimental.pallas import tpu as pltpu

D_MODEL = 1024
D_POOL = 256
D_RWKV = 768
HEAD_DIM = 64
N_HEADS = 12
LANES = 128
N_PAIRS = D_RWKV // LANES
D_SHIFT = 2560
D_FF = 2816
D_PLE = 256
POOL_WINDOWS = (2, 4, 8, 16)
POOL_HIST = 15
POOL_PAD = 16
DEPTH = 2
ALPHA = (2.0 * DEPTH) ** 0.25
LN_EPS = 1e-5
LNX_EPS = 64e-5
PAST_LEN = 1024
CHUNK = 64
N_STAGES = int(math.log2(CHUNK))
TOKEN_TILE = 512
SUB_TILE = TOKEN_TILE // 2
F32_SUBLANES = 8
MIX_ROWS = 4
V7X_VMEM_BYTES = 64 * 1024 * 1024
DENSE_VMEM_LIMIT = V7X_VMEM_BYTES * 7 // 8

F32 = jnp.float32
BF16 = jnp.bfloat16
_NT = (((1,), (1,)), ((), ()))
_TN = (((0,), (0,)), ((), ()))


def _dot(a, b):
    return jnp.dot(a, b, preferred_element_type=F32)


def _sigmoid(x):
    return 0.5 * jnp.tanh(0.5 * x) + 0.5


def _layer_norm(h, g, b, eps):
    mu = jnp.mean(h, axis=-1, keepdims=True)
    d = h - mu
    var = jnp.mean(d * d, axis=-1, keepdims=True)
    return d * lax.rsqrt(var + eps) * g + b


def _swiglu_halves(xbs, w_in_ref, w_out_ref):
    hs = [(_dot(xb, w_in_ref[:, :D_FF]), _dot(xb, w_in_ref[:, D_FF:])) for xb in xbs]
    return [_dot((gate * _sigmoid(gate) * up).astype(BF16), w_out_ref[...]) for gate, up in hs]


def _halves(ref):
    return [ref[0:SUB_TILE, :], ref[SUB_TILE:TOKEN_TILE, :]]


def _ffn1_kernel(x_ref, w_in_ref, w_out_ref, g_ref, b_ref, wz_ref, x1_ref, u_ref, zr_ref):
    xs = _halves(x_ref)
    ffn = _swiglu_halves([x.astype(BF16) for x in xs], w_in_ref, w_out_ref)
    for i, (x, f) in enumerate(zip(xs, ffn)):
        rows = slice(i * SUB_TILE, (i + 1) * SUB_TILE)
        x1 = _layer_norm(ALPHA * x + 0.5 * f, g_ref[...], b_ref[...], LN_EPS)
        x1_ref[rows, :] = x1
        z = _dot(x1.astype(BF16), wz_ref[...]).astype(BF16)
        u_ref[rows, :] = z[:, :D_POOL]
        zr_ref[rows, :] = z[:, D_POOL:]


def _post_kernel(mix_ref, x1_ref, p_ref, wo_ref, g2_ref, b2_ref, w_in_ref, w_out_ref,
                 wg_ref, bg_ref, wp_ref, g3_ref, b3_ref, out_ref):
    mixes = [_dot(m, wo_ref[...]) for m in _halves(mix_ref)]
    x2s = [_layer_norm(ALPHA * x1 + mix, g2_ref[...], b2_ref[...], LN_EPS) for x1, mix in zip(_halves(x1_ref), mixes)]
    ffn = _swiglu_halves([x2.astype(BF16) for x2 in x2s], w_in_ref, w_out_ref)
    ples = [_dot(p.astype(BF16), wp_ref[...]) for p in _halves(p_ref)]
    for i, (x2, f, ple) in enumerate(zip(x2s, ffn, ples)):
        h = ALPHA * x2 + 0.5 * f
        gate = _sigmoid(_dot(h.astype(BF16), wg_ref[...]) + bg_ref[...])
        out_ref[i * SUB_TILE:(i + 1) * SUB_TILE, :] = _layer_norm(h + gate * ple, g3_ref[...], b3_ref[...], LN_EPS)


def _mix_kernel(t_valid, pos0,
                u_ref, zr_ref, hist_ref, shift_ref, s0_ref,
                mu_ref, w0_ref, w2_ref, a0_ref, a2_ref, g2_ref, kk_ref, ka_ref, rk_ref,
                lng_ref, lnb_ref, pw_ref, ps_ref,
                out_ref, s_ref,
                uext_ref, carry_ref, prev_ref, opsb_ref, opsf_ref, pool_ref, pl_ref):
    c = pl.program_id(1)
    refs = (u_ref, zr_ref, mu_ref, w0_ref, w2_ref, a0_ref, a2_ref, g2_ref, kk_ref, ka_ref, rk_ref,
            lng_ref, lnb_ref, pw_ref, ps_ref, out_ref, s_ref,
            uext_ref, carry_ref, prev_ref, opsb_ref, opsf_ref, pool_ref, pl_ref)

    @pl.when(c == 0)
    def _():
        carry_ref[1] = hist_ref[...]
        prev_ref[1, :, F32_SUBLANES - 1:F32_SUBLANES, :] = shift_ref[...]
        s_ref[...] = s0_ref[...]
        opsb_ref[1] = jnp.zeros(opsb_ref.shape[1:], BF16)
        opsf_ref[1] = jnp.zeros(opsf_ref.shape[1:], F32)
        pool_ref[1] = jnp.zeros(pool_ref.shape[1:], F32)
        pl_ref[1] = jnp.ones(pl_ref.shape[1:], F32)

    @pl.when((c & 1) == 0)
    def _():
        _mix_step(t_valid, pos0, 1, 0, c, *refs)

    @pl.when((c & 1) == 1)
    def _():
        _mix_step(t_valid, pos0, 0, 1, c, *refs)


def _mix_step(t_valid, pos0, rd, wr, c,
              u_ref, zr_ref, mu_ref, w0_ref, w2_ref, a0_ref, a2_ref, g2_ref, kk_ref, ka_ref, rk_ref,
              lng_ref, lnb_ref, pw_ref, ps_ref, out_ref, s_ref,
              uext_ref, carry_ref, prev_ref, opsb_ref, opsf_ref, pool_ref, pl_ref):
    assert N_STAGES == 6, "the prepare/apply split below places solve stages 2-3 and 4-6 explicitly"
    L = CHUNK
    RL = MIX_ROWS * L
    row_l = lax.broadcasted_iota(jnp.int32, (L, 1), 0)
    row = lax.broadcasted_iota(jnp.int32, (RL, 1), 0)
    rin = row & (L - 1)
    lane = lax.broadcasted_iota(jnp.int32, (1, LANES), 1)
    left = lane < HEAD_DIM
    col = lane & (HEAD_DIM - 1)
    strict = col < row_l
    incl = col <= row_l
    eye = (col == row_l).astype(F32)
    sls = [slice(p * LANES, (p + 1) * LANES) for p in range(N_PAIRS)]
    rws = [slice(r * L, (r + 1) * L) for r in range(MIX_ROWS)]
    chains = [(rw, sl) for rw in rws for sl in sls]
    ids = range(len(chains))

    ones_bd = ((lax.broadcasted_iota(jnp.int32, (LANES, LANES), 0) < HEAD_DIM)
               == (lax.broadcasted_iota(jnp.int32, (LANES, LANES), 1) < HEAD_DIM)).astype(BF16)

    def head_sum(x):
        n = x.shape[0]
        tiles = jnp.concatenate([x[:, sl] for sl in sls], axis=0)
        sums = _dot(tiles.astype(BF16), ones_bd)
        return jnp.concatenate([sums[p * n:(p + 1) * n] for p in range(N_PAIRS)], axis=1)

    def bd(x):
        z = jnp.zeros_like(x)
        return jnp.concatenate([jnp.where(left, x, z), jnp.where(left, z, x)], axis=0)

    def pmm(a_, x_):
        return _dot(a_.astype(BF16), bd(x_.astype(BF16)))

    def assemble(tiles):
        return jnp.concatenate([jnp.concatenate(tiles[r * N_PAIRS:(r + 1) * N_PAIRS], axis=1)
                                for r in range(MIX_ROWS)], axis=0)

    def split(x):
        return [x[rw, sl] for rw, sl in chains]

    def solve_stage(stage, tm, pw):
        last = stage == N_STAGES
        prod = [_dot((tm[i] if last else jnp.concatenate([tm[i], pw[i]], axis=0)).astype(BF16),
                     bd(pw[i].astype(BF16))) for i in ids]
        tm = [tm[i] + prod[i][0:L] for i in ids]
        return tm, (None if last else [x[L:2 * L] for x in prod])

    at_o, rt_o = opsb_ref[rd, 0], opsb_ref[rd, 1]
    s_old = [s_ref[r][:, sl] for r in range(MIX_ROWS) for sl in sls]
    asrs = [lax.dot_general(jnp.concatenate([at_o[rw, sl], rt_o[rw, sl]], axis=0), bd(s_old[i].astype(BF16)),
                            _NT, preferred_element_type=F32) for i, (rw, sl) in enumerate(chains)]

    pooled = []
    for r in range(MIX_ROWS):
        u = u_ref[r].astype(F32)
        uext_ref[r, 0:POOL_PAD, :] = carry_ref[rd, r]
        uext_ref[r, POOL_PAD:POOL_PAD + L, :] = u
        carry_ref[wr, r] = u[L - POOL_PAD:L, :]
        run = u
        sums = {}
        for j in range(1, max(POOL_WINDOWS)):
            run = run + uext_ref[r, POOL_PAD - j:POOL_PAD - j + L, :]
            if j + 1 in POOL_WINDOWS:
                sums[j + 1] = run
        pos1 = pos0 + c * L + row_l + 1
        lane_u = lax.broadcasted_iota(jnp.int32, (1, D_POOL), 1)
        means = sums[POOL_WINDOWS[-1]] / jnp.minimum(pos1, POOL_WINDOWS[-1]).astype(F32)
        for gi in range(len(POOL_WINDOWS) - 2, -1, -1):
            w = POOL_WINDOWS[gi]
            means = jnp.where(lane_u < (gi + 1) * HEAD_DIM, sums[w] / jnp.minimum(pos1, w).astype(F32), means)
        pooled.append(means - u)
    pool_out = _dot(jnp.concatenate(pooled, axis=0).astype(BF16), pw_ref[...]) * ps_ref[...]

    zr = zr_ref[...].reshape(RL, D_SHIFT).astype(F32)
    last = slice(F32_SUBLANES - 1, F32_SUBLANES)
    rolled = pltpu.roll(zr, 1, 0)
    first = lax.broadcasted_iota(jnp.int32, (F32_SUBLANES, 1), 0) == 0
    parts = []
    for r in range(MIX_ROWS):
        parts += [jnp.where(first, prev_ref[rd, r, last, :], rolled[r * L:r * L + F32_SUBLANES]),
                  rolled[r * L + F32_SUBLANES:(r + 1) * L]]
    zprev = jnp.concatenate(parts, axis=0)
    for r in range(MIX_ROWS):
        prev_ref[wr, r] = zr[(r + 1) * L - F32_SUBLANES:(r + 1) * L, :]
    zs = zr + (zprev - zr) * mu_ref[...]
    r_ = zs[:, 0:D_RWKV]
    k = zs[:, D_RWKV:2 * D_RWKV]
    v = zs[:, 2 * D_RWKV:3 * D_RWKV]
    lwa = zs[:, 3 * D_RWKV:3 * D_RWKV + LANES]
    lg = zs[:, 3 * D_RWKV + LANES:]
    tl = jnp.where(left, jnp.tanh(lwa), lwa).astype(BF16)
    logw = -math.exp(-0.5) * _sigmoid(w0_ref[...] + _dot(tl, w2_ref[...]))
    a = _sigmoid(a0_ref[...] + _dot(tl, a2_ref[...]))
    g = _dot(_sigmoid(lg).astype(BF16), g2_ref[...])

    tm_o, pw_o = solve_stage(N_STAGES - 2, split(opsf_ref[rd, 0]), split(opsf_ref[rd, 1]))

    kkr = k * kk_ref[...]
    kk = kkr * jnp.minimum(jnp.exp(-0.5 * jnp.log(head_sum(kkr * kkr))), 1e12)
    k2 = k * (1.0 + (a - 1.0) * ka_ref[...])
    b = kk * a
    if t_valid < L:
        valid = rin < t_valid
        logw = jnp.where(valid, logw, 0.0)
        b = jnp.where(valid, b, 0.0)
        k2 = jnp.where(valid, k2, 0.0)
    ri = lax.broadcasted_iota(jnp.int32, (RL, RL), 0)
    ci = lax.broadcasted_iota(jnp.int32, (RL, RL), 1)
    same_row = (ri & -L) == (ci & -L)
    sel = jnp.concatenate([(ri >= ci) & same_row, same_row], axis=0).astype(BF16)
    hi = logw.astype(BF16)
    lo = (logw - hi.astype(F32)).astype(BF16)
    sums = _dot(sel, hi) + _dot(sel, lo)
    cum = sums[0:RL]
    cum_lb = sums[RL:]

    tm_o, pw_o = solve_stage(N_STAGES - 1, tm_o, pw_o)

    e_cum = jnp.exp(cum)
    e_inv = jnp.exp(-cum)
    e_prev = jnp.where(rin == 0, 1.0, pltpu.roll(e_cum, 1, 0))
    p_new = [jnp.exp(cum_lb[r * L:r * L + F32_SUBLANES, :]) for r in range(MIX_ROWS)]
    e_tail = jnp.concatenate([x for x in p_new for _ in range(L // F32_SUBLANES)], axis=0) * e_inv
    at = (-kk * e_prev).astype(BF16)
    rt = (r_ * e_cum).astype(BF16)
    bt = (b * e_inv).astype(BF16)
    kt = (k2 * e_inv).astype(BF16)
    vb = v.astype(BF16)
    sc = [lax.dot_general(jnp.concatenate([at[rw, sl], rt[rw, sl]], axis=0),
                          jnp.concatenate([bd(bt[rw, sl]), bd(kt[rw, sl])], axis=0),
                          _NT, preferred_element_type=F32) for rw, sl in chains]
    n_ = [jnp.where(strict, x[0:L, 0:2 * L], 0.0) for x in sc]
    wb = [jnp.where(incl, x[L:2 * L, 0:2 * L], 0.0) for x in sc]
    m_ = [jnp.where(strict, x[0:L, 2 * L:4 * L], 0.0) for x in sc]
    wk = [jnp.where(incl, x[L:2 * L, 2 * L:4 * L], 0.0) for x in sc]

    tm_o, _ = solve_stage(N_STAGES, tm_o, pw_o)

    v_n = split(vb)
    mwv = [_dot(jnp.concatenate([m_[i], wk[i]], axis=0).astype(BF16), bd(v_n[i])) for i in ids]
    tm = [eye + x for x in n_]
    pw = [pmm(x, x) for x in n_]

    mv_o, wkv_o = split(opsf_ref[rd, 2]), split(opsf_ref[rd, 3])
    u_b = [pmm(tm_o[i], asrs[i][0:L] + mv_o[i]).astype(BF16) for i in ids]

    bonus_v = head_sum(r_ * k2 * rk_ref[...]) * v
    tm, pw = solve_stage(2, tm, pw)

    wb_o = split(opsb_ref[rd, 5])
    ys = [asrs[i][L:2 * L] + wkv_o[i] + pmm(wb_o[i], u_b[i]) for i in ids]
    bh_o, kh_o, v_o = opsb_ref[rd, 2], opsb_ref[rd, 3], opsb_ref[rd, 4]
    full = [lax.dot_general(jnp.concatenate([u_b[i], v_o[rw, sl]], axis=0),
                            jnp.concatenate([bh_o[rw, sl], kh_o[rw, sl]], axis=0),
                            _TN, preferred_element_type=F32) for i, (rw, sl) in enumerate(chains)]
    for r in range(MIX_ROWS):
        p_l = pl_ref[rd, r, 0:1, :]
        s_ref[r] = jnp.concatenate(
            [s_old[r * N_PAIRS + p] * p_l[:, sl]
             + jnp.where(left, full[r * N_PAIRS + p][0:HEAD_DIM], full[r * N_PAIRS + p][HEAD_DIM:])
             for p, sl in enumerate(sls)], axis=1)
    y = assemble(ys)

    tm, pw = solve_stage(3, tm, pw)

    mean = head_sum(y) * (1.0 / HEAD_DIM)
    yc = y - mean
    var = head_sum(yc * yc) * (1.0 / HEAD_DIM)
    yn = yc * lax.rsqrt(var + LNX_EPS) * lng_ref[...] + lnb_ref[...]
    rwkv_out = (yn + opsf_ref[rd, 4]) * opsf_ref[rd, 5]
    out_ref[...] = jnp.concatenate([pool_ref[rd], rwkv_out], axis=1).astype(BF16).reshape(MIX_ROWS, L, D_MODEL)

    new_b = [at, rt, (b * e_tail).astype(BF16), (k2 * e_tail).astype(BF16), vb, assemble(wb).astype(BF16)]
    new_f = [assemble(tm), assemble(pw), assemble([x[0:L] for x in mwv]), assemble([x[L:2 * L] for x in mwv]),
             bonus_v, g]
    for i, x in enumerate(new_b):
        opsb_ref[wr, i] = x
    for i, x in enumerate(new_f):
        opsf_ref[wr, i] = x
    pool_ref[wr] = pool_out
    for r in range(MIX_ROWS):
        pl_ref[wr, r] = p_new[r]


def _const_spec(shape):
    return pl.BlockSpec(shape, lambda *_: (0,) * len(shape), pipeline_mode=pl.Buffered(1))


def _row_spec(width):
    return pl.BlockSpec((TOKEN_TILE, width), lambda i: (i, 0))


def _ffn1_call(x, lw):
    n = x.shape[0]
    consts = (lw["ffn1_w_in"], lw["ffn1_w_out"], lw["ln1_g"], lw["ln1_b"], lw["w_in"])
    return pl.pallas_call(
        _ffn1_kernel,
        grid=(n // TOKEN_TILE,),
        in_specs=[_row_spec(D_MODEL)] + [_const_spec(c.shape) for c in consts],
        out_specs=[_row_spec(D_MODEL), _row_spec(D_POOL), _row_spec(D_SHIFT)],
        out_shape=[jax.ShapeDtypeStruct((n, D_MODEL), F32),
                   jax.ShapeDtypeStruct((n, D_POOL), BF16),
                   jax.ShapeDtypeStruct((n, D_SHIFT), BF16)],
        compiler_params=pltpu.CompilerParams(dimension_semantics=("parallel",),
                                             vmem_limit_bytes=DENSE_VMEM_LIMIT),
        name="ffn1",
    )(x, *consts)


def _post_call(mix, x1, p, layer, lw):
    n = x1.shape[0]
    consts = (lw["w_out"], lw["ln2_g"], lw["ln2_b"], lw["ffn2_w_in"], lw["ffn2_w_out"],
              lw["ple_gate_w"], lw["ple_gate_b"], lw["ple_w"], lw["ln3_g"], lw["ln3_b"])
    return pl.pallas_call(
        _post_kernel,
        grid=(n // TOKEN_TILE,),
        in_specs=[_row_spec(D_MODEL), _row_spec(D_MODEL),
                  pl.BlockSpec((None, TOKEN_TILE, D_PLE), lambda i: (layer, i, 0))]
        + [_const_spec(c.shape) for c in consts],
        out_specs=_row_spec(D_MODEL),
        out_shape=jax.ShapeDtypeStruct((n, D_MODEL), F32),
        compiler_params=pltpu.CompilerParams(dimension_semantics=("parallel",),
                                             vmem_limit_bytes=DENSE_VMEM_LIMIT),
        name="post",
    )(mix, x1, p, *consts)


def _mix_call(u, zr, hist, shift, s0, lw, t_valid, pos0):
    bn, tp, _ = u.shape
    nc = tp // CHUNK
    consts = (lw["mu_shift"], lw["w0"], lw["w2"], lw["a0"], lw["a2"], lw["g2"], lw["k_k"], lw["k_a"],
              lw["r_k"], lw["lnx_g"], lw["lnx_b"], lw["pool_w"], lw["pool_scale"])
    rl = MIX_ROWS * CHUNK
    per_row = lambda shape: pl.BlockSpec((MIX_ROWS,) + shape, lambda bi, ci: (bi, 0, 0))
    chunk_in = lambda width: pl.BlockSpec((MIX_ROWS, CHUNK, width), lambda bi, ci: (bi, jnp.minimum(ci, nc - 1), 0))
    chunk_out = lambda width: pl.BlockSpec((MIX_ROWS, CHUNK, width), lambda bi, ci: (bi, jnp.maximum(ci - 1, 0), 0))
    return pl.pallas_call(
        functools.partial(_mix_kernel, t_valid, pos0),
        grid=(bn // MIX_ROWS, nc + 1),
        in_specs=[chunk_in(D_POOL), chunk_in(D_SHIFT), per_row((POOL_PAD, D_POOL)),
                  per_row((1, D_SHIFT)), per_row((HEAD_DIM, D_RWKV))]
        + [_const_spec(c.shape) for c in consts],
        out_specs=[chunk_out(D_MODEL), per_row((HEAD_DIM, D_RWKV))],
        out_shape=[jax.ShapeDtypeStruct((bn, tp, D_MODEL), BF16),
                   jax.ShapeDtypeStruct((bn, HEAD_DIM, D_RWKV), F32)],
        scratch_shapes=[pltpu.VMEM((MIX_ROWS, POOL_PAD + CHUNK, D_POOL), F32),
                        pltpu.VMEM((2, MIX_ROWS, POOL_PAD, D_POOL), F32),
                        pltpu.VMEM((2, MIX_ROWS, F32_SUBLANES, D_SHIFT), F32),
                        pltpu.VMEM((2, 6, rl, D_RWKV), BF16),
                        pltpu.VMEM((2, 6, rl, D_RWKV), F32),
                        pltpu.VMEM((2, rl, D_POOL), F32),
                        pltpu.VMEM((2, MIX_ROWS, F32_SUBLANES, D_RWKV), F32)],
        compiler_params=pltpu.CompilerParams(dimension_semantics=("parallel", "arbitrary")),
        name="mix",
    )(u, zr, hist, shift, s0, *consts)


def _prepare_layer(i, W):
    row = lambda name: W[name][i].reshape(1, -1).astype(F32)
    mat = lambda name: W[name][i].astype(BF16)
    zeros = jnp.zeros((HEAD_DIM, D_RWKV), BF16)
    lw = {name: mat(name) for name in ("ffn1_w_in", "ffn1_w_out", "ffn2_w_in", "ffn2_w_out", "w_in", "g2",
                                       "w_out", "ple_gate_w", "ple_w")}
    lw.update({name: row(name) for name in ("ln1_g", "ln1_b", "ln2_g", "ln2_b", "ln3_g", "ln3_b", "mu_shift",
                                            "pool_scale", "w0", "a0", "k_k", "k_a", "r_k", "lnx_g", "lnx_b",
                                            "ple_gate_b")})
    lw["w2"] = jnp.concatenate([mat("w2"), zeros], axis=0)
    lw["a2"] = jnp.concatenate([zeros, mat("a2")], axis=0)
    lw["pool_w"] = jax.scipy.linalg.block_diag(*[W["pool_w"][i, gi] for gi in range(len(POOL_WINDOWS))]).astype(BF16)
    return lw


def _run_trunk(x, p, pos0, pool_hist, shift_prev, wkv_prev, layers):
    bn, t, _ = x.shape
    n = bn * t
    tp = -(-t // CHUNK) * CHUNK
    assert tp == t or t < CHUNK, "time padding is only supported for a single partial chunk"
    xf = x.reshape(n, D_MODEL)
    pools, shifts, wkvs = [], [], []
    for i, lw in enumerate(layers):
        x1, u, zr = _ffn1_call(xf, lw)
        u = u.reshape(bn, t, D_POOL)
        zr = zr.reshape(bn, t, D_SHIFT)
        hist = jnp.zeros((bn, POOL_HIST, D_POOL), F32) if pool_hist is None else pool_hist[i]
        pools.append(jnp.concatenate([hist, u[:, -POOL_HIST:]], axis=1)[:, -POOL_HIST:])
        shifts.append(zr[:, -1].astype(F32))
        if tp != t:
            u = jnp.pad(u, ((0, 0), (0, tp - t), (0, 0)))
            zr = jnp.pad(zr, ((0, 0), (0, tp - t), (0, 0)))
        hist = jnp.pad(hist, ((0, 0), (POOL_PAD - POOL_HIST, 0), (0, 0)))
        s0 = jnp.transpose(wkv_prev[i], (0, 2, 1, 3)).reshape(bn, HEAD_DIM, D_RWKV)
        mix, s_new = _mix_call(u, zr, hist, shift_prev[i][:, None, :], s0, lw, min(t, CHUNK), pos0)
        wkvs.append(jnp.transpose(s_new.reshape(bn, HEAD_DIM, N_HEADS, HEAD_DIM), (0, 2, 1, 3)))
        xf = _post_call(mix[:, :t].reshape(n, D_MODEL), x1, p.reshape(DEPTH, n, D_PLE), i, lw)
    return xf.reshape(bn, t, D_MODEL), jnp.stack(pools), jnp.stack(shifts), jnp.stack(wkvs)


def kernel(x_prompt, x_sample, p_prompt, p_sample, state_pool, state_shift, state_wkv, ln1_g, ln1_b, ln2_g, ln2_b, ln3_g, ln3_b, ffn1_w_in, ffn1_w_out, ffn2_w_in, ffn2_w_out, w_in, mu_shift, pool_w, pool_scale, w0, w2, a0, a2, g2, k_k, k_a, r_k, lnx_g, lnx_b, w_out, ple_gate_w, ple_gate_b, ple_w):
    W = dict(ln1_g=ln1_g, ln1_b=ln1_b, ln2_g=ln2_g, ln2_b=ln2_b, ln3_g=ln3_g, ln3_b=ln3_b,
             ffn1_w_in=ffn1_w_in, ffn1_w_out=ffn1_w_out, ffn2_w_in=ffn2_w_in, ffn2_w_out=ffn2_w_out,
             w_in=w_in, mu_shift=mu_shift, pool_w=pool_w, pool_scale=pool_scale, w0=w0, w2=w2,
             a0=a0, a2=a2, g2=g2, k_k=k_k, k_a=k_a, r_k=r_k, lnx_g=lnx_g, lnx_b=lnx_b,
             w_out=w_out, ple_gate_w=ple_gate_w, ple_gate_b=ple_gate_b, ple_w=ple_w)
    layers = [_prepare_layer(i, W) for i in range(DEPTH)]
    bn = x_prompt.shape[0]
    y_prompt, pool_prompt, shift_prompt, wkv_prompt = _run_trunk(
        x_prompt, p_prompt, 0, None,
        jnp.zeros((DEPTH, bn, D_SHIFT), F32),
        jnp.zeros((DEPTH, bn, N_HEADS, HEAD_DIM, HEAD_DIM), F32), layers)
    y_sample, pool_sample, shift_sample, wkv_sample = _run_trunk(
        x_sample, p_sample, PAST_LEN, state_pool, state_shift, state_wkv, layers)
    return (y_prompt, y_sample, pool_prompt, shift_prompt, wkv_prompt, pool_sample, shift_sample, wkv_sample)
```
